```python
import math
import jax, jax.numpy as jnp
from jax import lax
import numpy as np

D_MODEL = 2048
BATCH = 16
SEQ = 2048
DEPTH = 1

D_MLSTM = D_MODEL // 2
N_MLSTM_HEADS = 4
DV_HEAD = D_MLSTM // N_MLSTM_HEADS
DQK_HEAD = DV_HEAD // 2
D_QK = N_MLSTM_HEADS * DQK_HEAD
MLSTM_CHUNK = 128
CONV_WIDTH = 4
D_GMLP = D_MODEL - D_MLSTM
N_GMLP_GROUPS = 8
GMLP_GROUP = D_GMLP // N_GMLP_GROUPS
SPATIAL_CHUNK = 128
D_FF = 5632
N_SUBLAYERS = 3
N_MOD = 3
EPS = 1e-6
D_IN = 2 * D_QK + 2 * D_MLSTM + 2 * N_MLSTM_HEADS + 2 * D_GMLP
SPLITS = tuple(np.cumsum([D_QK, D_QK, D_MLSTM, D_MLSTM, N_MLSTM_HEADS, N_MLSTM_HEADS, D_GMLP])[:].tolist())

kernel_name = "hymba_style_mlstm_gmlp_macaron_adaln"


def rms_norm(x, g):
    xf = x.astype(jnp.float32)
    y = xf * lax.rsqrt(jnp.mean(xf * xf, axis=-1, keepdims=True) + EPS)
    return (y * g.astype(jnp.float32)).astype(x.dtype)


def layer_norm(x, g, b):
    xf = x.astype(jnp.float32)
    mu = jnp.mean(xf, axis=-1, keepdims=True)
    var = jnp.mean(jnp.square(xf - mu), axis=-1, keepdims=True)
    y = (xf - mu) * lax.rsqrt(var + EPS)
    return (y * g.astype(jnp.float32) + b.astype(jnp.float32)).astype(x.dtype)


def causal_dwconv(x, w, b):
    s = x.shape[1]
    xp = jnp.pad(x, ((0, 0), (CONV_WIDTH - 1, 0), (0, 0)))
    y = sum(w[j] * xp[:, j:j + s] for j in range(CONV_WIDTH))
    return y + b


def swiglu(h, w_gate, w_up, w_down):
    return (jax.nn.silu(h @ w_gate) * (h @ w_up)) @ w_down


def mlstm_chunkwise(q, k, v, i_pre, f_pre):
    b_, s_, h_, _ = q.shape
    nc, L = s_ // MLSTM_CHUNK, MLSTM_CHUNK

    def to_chunks(t):
        return t.astype(jnp.float32).reshape(b_, nc, L, h_, t.shape[-1]).transpose(0, 3, 1, 2, 4)

    def gate_chunks(t):
        return t.astype(jnp.float32).reshape(b_, nc, L, h_).transpose(0, 3, 1, 2)

    qc, kc, vc = to_chunks(q), to_chunks(k), to_chunks(v)
    ig = gate_chunks(i_pre)
    logf = jax.nn.log_sigmoid(gate_chunks(f_pre))
    bcum = jnp.cumsum(logf, axis=-1)
    b_last = bcum[..., -1]

    causal = jnp.tril(jnp.ones((L, L), dtype=bool))
    dmat = jnp.where(causal, bcum[..., :, None] - bcum[..., None, :] + ig[..., None, :], -jnp.inf)
    m_intra = jnp.max(dmat, axis=-1)

    w_state = b_last[..., None] - bcum + ig
    m_loc = jnp.max(w_state, axis=-1)
    e_state = jnp.exp(w_state - m_loc[..., None])
    c_inc = jnp.einsum('bhnl,bhnle,bhnld->bhned', e_state, vc, kc)
    n_inc = jnp.einsum('bhnl,bhnld->bhnd', e_state, kc)

    def step(carry, xs):
        c_st, n_st, m_st = carry
        c_x, n_x, m_x, b_x = xs
        m_new = jnp.maximum(b_x + m_st, m_x)
        a = jnp.exp(b_x + m_st - m_new)
        s = jnp.exp(m_x - m_new)
        c_new = a[..., None, None] * c_st + s[..., None, None] * c_x
        n_new = a[..., None] * n_st + s[..., None] * n_x
        return (c_new, n_new, m_new), (c_st, n_st, m_st)

    init = (jnp.zeros((b_, h_, DV_HEAD, DQK_HEAD), jnp.float32),
            jnp.zeros((b_, h_, DQK_HEAD), jnp.float32),
            jnp.zeros((b_, h_), jnp.float32))
    xs = (jnp.moveaxis(c_inc, 2, 0), jnp.moveaxis(n_inc, 2, 0),
          jnp.moveaxis(m_loc, 2, 0), jnp.moveaxis(b_last, 2, 0))
    _, (c_prev, n_prev, m_prev) = lax.scan(step, init, xs)
    c_prev = jnp.moveaxis(c_prev, 0, 2)
    n_prev = jnp.moveaxis(n_prev, 0, 2)
    m_prev = jnp.moveaxis(m_prev, 0, 2)

    m_inter = bcum + m_prev[..., None]
    m_t = jnp.maximum(m_inter, m_intra)
    inter_scale = jnp.exp(m_inter - m_t)
    p = jnp.exp(dmat - m_t[..., None])
    s_qk = jnp.einsum('bhnld,bhnsd->bhnls', qc, kc) * p
    num = (inter_scale[..., None] * jnp.einsum('bhnld,bhned->bhnle', qc, c_prev)
           + jnp.einsum('bhnls,bhnse->bhnle', s_qk, vc))
    den = inter_scale * jnp.einsum('bhnld,bhnd->bhnl', qc, n_prev) + jnp.sum(s_qk, axis=-1)
    hc = num / jnp.maximum(jnp.abs(den), jnp.exp(-m_t))[..., None]
    return hc.transpose(0, 2, 3, 1, 4).reshape(b_, s_, h_, DV_HEAD)


def spatial_gating(u, v, ln_g, ln_b, w_sp, b_sp):
    b_, s_, _ = v.shape
    nc, L = s_ // SPATIAL_CHUNK, SPATIAL_CHUNK
    v = layer_norm(v, ln_g, ln_b).reshape(b_, nc, L, N_GMLP_GROUPS, GMLP_GROUP)
    w_causal = w_sp * jnp.tril(jnp.ones((L, L), dtype=w_sp.dtype))
    z = jnp.einsum('gts,bnsgc->bntgc', w_causal, v) + b_sp.T[None, None, :, :, None]
    return u * z.reshape(b_, s_, D_GMLP)


def setup_inputs(seed: int = 0) -> dict:
    key = jax.random.key(seed)
    ks = jax.random.split(key, 24)
    f32 = jnp.float32
    nrm = lambda k, shape, scale: jax.random.normal(k, shape, f32) * scale
    return {
        "x": nrm(ks[0], (BATCH, SEQ, D_MODEL), 1.0),
        "c": nrm(ks[1], (BATCH, D_MODEL), 1.0),
        "w_ada": nrm(ks[2], (DEPTH, D_MODEL, N_SUBLAYERS * N_MOD * D_MODEL), 0.5 * D_MODEL ** -0.5),
        "b_ada": nrm(ks[3], (DEPTH, N_SUBLAYERS * N_MOD * D_MODEL), 0.01),
        "g_pre": 1.0 + nrm(ks[4], (DEPTH, N_SUBLAYERS, D_MODEL), 0.05),
        "g_post": 1.0 + nrm(ks[5], (DEPTH, N_SUBLAYERS, D_MODEL), 0.05),
        "w_ff_gate": nrm(ks[6], (DEPTH, 2, D_MODEL, D_FF), D_MODEL ** -0.5),
        "w_ff_up": nrm(ks[7], (DEPTH, 2, D_MODEL, D_FF), D_MODEL ** -0.5),
        "w_ff_down": nrm(ks[8], (DEPTH, 2, D_FF, D_MODEL), D_FF ** -0.5),
        "w_in": nrm(ks[9], (DEPTH, D_MODEL, D_IN), D_MODEL ** -0.5),
        "conv_w": nrm(ks[10], (DEPTH, CONV_WIDTH, 2 * D_QK), CONV_WIDTH ** -0.5),
        "conv_b": nrm(ks[11], (DEPTH, 2 * D_QK), 0.01),
        "b_igate": nrm(ks[12], (DEPTH, N_MLSTM_HEADS), 0.1),
        "b_fgate": jnp.linspace(3.0, 6.0, N_MLSTM_HEADS, dtype=f32)[None, :] + nrm(ks[13], (DEPTH, N_MLSTM_HEADS), 0.1),
        "g_mhnorm": 1.0 + nrm(ks[14], (DEPTH, N_MLSTM_HEADS, DV_HEAD), 0.05),
        "gmlp_ln_g": 1.0 + nrm(ks[15], (DEPTH, D_GMLP), 0.05),
        "gmlp_ln_b": nrm(ks[16], (DEPTH, D_GMLP), 0.01),
        "w_spatial": nrm(ks[17], (DEPTH, N_GMLP_GROUPS, SPATIAL_CHUNK, SPATIAL_CHUNK), SPATIAL_CHUNK ** -0.5),
        "b_spatial": 1.0 + nrm(ks[18], (DEPTH, N_GMLP_GROUPS, SPATIAL_CHUNK), 0.05),
        "w_out": nrm(ks[19], (DEPTH, D_MODEL, D_MODEL), D_MODEL ** -0.5),
    }


def reference(x, c, w_ada, b_ada, g_pre, g_post, w_ff_gate, w_ff_up, w_ff_down, w_in, conv_w, conv_b,
              b_igate, b_fgate, g_mhnorm, gmlp_ln_g, gmlp_ln_b, w_spatial, b_spatial, w_out):
    b_, s_, _ = x.shape
    for l in range(DEPTH):
        mod = (jax.nn.silu(c) @ w_ada[l] + b_ada[l]).reshape(b_, N_SUBLAYERS, N_MOD, D_MODEL)

        def sublayer(x_res, j, fn, coef):
            shift, scale, gate = mod[:, j, 0], mod[:, j, 1], mod[:, j, 2]
            h = rms_norm(x_res, g_pre[l, j]) * (1.0 + scale[:, None, :]) + shift[:, None, :]
            y = rms_norm(fn(h), g_post[l, j])
            return x_res + coef * gate[:, None, :] * y

        def ffn(idx):
            return lambda h: swiglu(h, w_ff_gate[l, idx], w_ff_up[l, idx], w_ff_down[l, idx])

        def mixer(h):
            proj = h @ w_in[l]
            q, k, v, o, ig, fg, u, gv = jnp.split(proj, SPLITS, axis=-1)
            qk = jax.nn.silu(causal_dwconv(jnp.concatenate([q, k], axis=-1), conv_w[l], conv_b[l]))
            q, k = qk[..., :D_QK], qk[..., D_QK:]
            q = q.reshape(b_, s_, N_MLSTM_HEADS, DQK_HEAD)
            k = k.reshape(b_, s_, N_MLSTM_HEADS, DQK_HEAD) * (DQK_HEAD ** -0.5)
            v = v.reshape(b_, s_, N_MLSTM_HEADS, DV_HEAD)
            hm = mlstm_chunkwise(q, k, v, ig + b_igate[l], fg + b_fgate[l])
            hm = rms_norm(hm, g_mhnorm[l]).astype(x.dtype).reshape(b_, s_, D_MLSTM)
            hm = jax.nn.sigmoid(o) * hm
            z = spatial_gating(jax.nn.gelu(u), jax.nn.gelu(gv), gmlp_ln_g[l], gmlp_ln_b[l],
                               w_spatial[l], b_spatial[l])
            return jnp.concatenate([hm, z], axis=-1) @ w_out[l]

        x = sublayer(x, 0, ffn(0), 0.5)
        x = sublayer(x, 1, mixer, 1.0)
        x = sublayer(x, 2, ffn(1), 0.5)
    return x
```

```python
import functools

import jax
import jax.numpy as jnp
from jax import lax
from jax.experimental import pallas as pl
from jax.experimental.pallas import tpu as pltpu

D_MODEL = 2048
D_MLSTM = D_MODEL // 2
N_HEADS = 4
DV = D_MLSTM // N_HEADS
DQK = DV // 2
D_QK = N_HEADS * DQK
CHUNK = 128
CONV_WIDTH = 4
D_GMLP = D_MODEL - D_MLSTM
N_GROUPS = 8
GROUP = D_GMLP // N_GROUPS
D_FF = 5632
N_SUB = 3
N_MOD = 3
EPS = 1e-6
GATE_LO = 2 * D_QK + 2 * D_MLSTM
GATE_HI = GATE_LO + 2 * N_HEADS
D_MAIN = 2 * D_QK + 2 * D_MLSTM + 2 * D_GMLP
LANES = 128
SUBLANES = 8
DV_AUG = DV + LANES

VMEM_LIMIT = 56 * 1024 * 1024

F32 = jnp.float32
BF16 = jnp.bfloat16


def _rms(y):
    return y * lax.rsqrt(jnp.mean(y * y, axis=-1, keepdims=True) + EPS)


def _ada_kernel(c_ref, w_ref, b_ref, o_ref):
    a = jax.nn.silu(c_ref[...]).astype(BF16)
    o_ref[...] = jnp.dot(a, w_ref[...].astype(BF16), preferred_element_type=F32) + b_ref[...]


def _ada_call(c, w, b, tn=1024):
    bsz, d = c.shape
    n = w.shape[1]
    return pl.pallas_call(
        _ada_kernel,
        grid=(n // tn,),
        in_specs=[
            pl.BlockSpec((bsz, d), lambda j: (0, 0)),
            pl.BlockSpec((d, tn), lambda j: (0, j)),
            pl.BlockSpec((1, tn), lambda j: (0, j)),
        ],
        out_specs=pl.BlockSpec((bsz, tn), lambda j: (0, j)),
        out_shape=jax.ShapeDtypeStruct((bsz, n), F32),
        compiler_params=pltpu.CompilerParams(
            dimension_semantics=("arbitrary",), vmem_limit_bytes=VMEM_LIMIT),
        name="adaln_mod",
    )(c, w, b.reshape(1, n))


def _ffn_kernel(x_ref, shift_ref, scale_ref, gate_ref, gpre_ref, gpost_ref,
                wg_ref, wu_ref, wd_ref, o_ref, h_ref, acc_ref, *, coef):
    j = pl.program_id(1)

    @pl.when(j == 0)
    def _():
        x = x_ref[...]
        h = _rms(x) * gpre_ref[...] * (1.0 + scale_ref[...]) + shift_ref[...]
        h_ref[...] = h.astype(BF16)

    h = h_ref[...]
    g = jnp.dot(h, wg_ref[...], preferred_element_type=F32)
    u = jnp.dot(h, wu_ref[...], preferred_element_type=F32)
    a = (jax.nn.silu(g) * u).astype(BF16)
    part = jnp.dot(a, wd_ref[...], preferred_element_type=F32)

    @pl.when(j == 0)
    def _():
        acc_ref[...] = part

    @pl.when(j > 0)
    def _():
        acc_ref[...] += part

    @pl.when(j == pl.num_programs(1) - 1)
    def _():
        y = _rms(acc_ref[...]) * gpost_ref[...]
        o_ref[...] = x_ref[...] + coef * gate_ref[...] * y


def _ffn_call(x, shift, scale, gate, g_pre, g_post, wg, wu, wd, coef, seq, tm=512, tf=512):
    t, d = x.shape
    ff = wg.shape[1]
    per_b = seq // tm
    row = lambda i, j: (i, 0)
    modm = lambda i, j: (i // per_b, 0, 0)
    const = lambda i, j: (0, 0)
    return pl.pallas_call(
        functools.partial(_ffn_kernel, coef=coef),
        grid=(t // tm, ff // tf),
        in_specs=[
            pl.BlockSpec((tm, d), row),
            pl.BlockSpec((None, 1, d), modm),
            pl.BlockSpec((None, 1, d), modm),
            pl.BlockSpec((None, 1, d), modm),
            pl.BlockSpec((1, d), const),
            pl.BlockSpec((1, d), const),
            pl.BlockSpec((d, tf), lambda i, j: (0, j)),
            pl.BlockSpec((d, tf), lambda i, j: (0, j)),
            pl.BlockSpec((tf, d), lambda i, j: (j, 0)),
        ],
        out_specs=pl.BlockSpec((tm, d), row),
        out_shape=jax.ShapeDtypeStruct((t, d), F32),
        scratch_shapes=[pltpu.VMEM((tm, d), BF16), pltpu.VMEM((tm, d), F32)],
        compiler_params=pltpu.CompilerParams(
            dimension_semantics=("parallel", "arbitrary"), vmem_limit_bytes=VMEM_LIMIT),
        name="ffn_sublayer",
    )(x, shift, scale, gate, g_pre, g_post, wg, wu, wd)


def _inproj_kernel(x_ref, shift_ref, scale_ref, gpre_ref, w_ref, wgate_ref,
                   o_ref, ogate_ref, h_ref):
    j = pl.program_id(1)

    @pl.when(j == 0)
    def _():
        h = _rms(x_ref[...]) * gpre_ref[...] * (1.0 + scale_ref[...]) + shift_ref[...]
        hb = h.astype(BF16)
        h_ref[...] = hb
        ogate_ref[...] = jnp.dot(hb, wgate_ref[...], preferred_element_type=F32)

    o_ref[...] = jnp.dot(h_ref[...], w_ref[...], preferred_element_type=F32).astype(BF16)


def _inproj_call(x, shift, scale, g_pre, w_main, w_gate, seq, tm=1024, tn=1024):
    t, d = x.shape
    n = w_main.shape[1]
    per_b = seq // tm
    modm = lambda i, j: (i // per_b, 0, 0)
    return pl.pallas_call(
        _inproj_kernel,
        grid=(t // tm, n // tn),
        in_specs=[
            pl.BlockSpec((tm, d), lambda i, j: (i, 0)),
            pl.BlockSpec((None, 1, d), modm),
            pl.BlockSpec((None, 1, d), modm),
            pl.BlockSpec((1, d), lambda i, j: (0, 0)),
            pl.BlockSpec((d, tn), lambda i, j: (0, j)),
            pl.BlockSpec((d, LANES), lambda i, j: (0, 0)),
        ],
        out_specs=[
            pl.BlockSpec((tm, tn), lambda i, j: (i, j)),
            pl.BlockSpec((tm, LANES), lambda i, j: (i, 0)),
        ],
        out_shape=[jax.ShapeDtypeStruct((t, n), BF16), jax.ShapeDtypeStruct((t, LANES), F32)],
        scratch_shapes=[pltpu.VMEM((tm, d), BF16)],
        compiler_params=pltpu.CompilerParams(
            dimension_semantics=("parallel", "arbitrary"), vmem_limit_bytes=VMEM_LIMIT),
        name="mixer_inproj",
    )(x, shift, scale, g_pre, w_main, w_gate)


def _mixer_kernel(qk_ref, v_ref, o_ref, u_ref, gv_ref, gcol_ref, grow_ref, x_ref, gate_ref,
                  convw_ref, convb_ref, bcol_ref, brow_ref, gmh_ref, lng_ref, lnb_ref,
                  wsp_ref, bsp_ref, wout_ref, gpost_ref,
                  out_ref,
                  qkbuf, q_s, k_s, vaug_s, z_s, wc_s, c_s, m_s, *, ts):
    s_idx = pl.program_id(1)
    n_chunks = ts // CHUNK
    tail = SUBLANES

    @pl.when(s_idx == 0)
    def _():
        qkbuf[0:tail, :] = jnp.zeros((tail, 2 * D_QK), F32)
        c_s[...] = jnp.zeros_like(c_s)
        m_s[...] = jnp.zeros_like(m_s)
        row = lax.broadcasted_iota(jnp.int32, (CHUNK, CHUNK), 0)
        col = lax.broadcasted_iota(jnp.int32, (CHUNK, CHUNK), 1)
        for g in range(N_GROUPS):
            wc_s[g] = jnp.where(col <= row, wsp_ref[g], 0.0).astype(BF16)

    qkbuf[tail:tail + ts, :] = qk_ref[...].astype(F32)
    for c in range(n_chunks):
        base = tail + c * CHUNK - (CONV_WIDTH - 1)
        acc = convb_ref[...] + convw_ref[0:1, :] * qkbuf[base:base + CHUNK, :]
        for j in range(1, CONV_WIDTH):
            acc = acc + convw_ref[j:j + 1, :] * qkbuf[base + j:base + j + CHUNK, :]
        act = jax.nn.silu(acc)
        q_s[c * CHUNK:(c + 1) * CHUNK, :] = act[:, :D_QK].astype(BF16)
        k_s[c * CHUNK:(c + 1) * CHUNK, :] = act[:, D_QK:] * (DQK ** -0.5)
    qkbuf[0:tail, :] = qkbuf[ts:ts + tail, :]

    for h in range(N_HEADS):
        vaug_s[:, h * DV_AUG:h * DV_AUG + DV] = v_ref[:, h * DV:(h + 1) * DV]
        vaug_s[:, h * DV_AUG + DV:(h + 1) * DV_AUG] = jnp.ones((ts, LANES), BF16)

    row = lax.broadcasted_iota(jnp.int32, (CHUNK, CHUNK), 0)
    col = lax.broadcasted_iota(jnp.int32, (CHUNK, CHUNK), 1)
    causal = col <= row
    tril = jnp.where(causal, 1.0, 0.0).astype(F32)
    triu = jnp.where(row <= col, 1.0, 0.0).astype(F32)

    def chunk_body(c, carry):
        r0 = pl.multiple_of(c * CHUNK, CHUNK)
        rows = pl.ds(r0, CHUNK)

        g_col = gcol_ref[rows, :] + bcol_ref[...]
        g_row = grow_ref[c] + brow_ref[...]
        bcum_col = jnp.dot(tril, jax.nn.log_sigmoid(g_col), precision=lax.Precision.HIGHEST,
                           preferred_element_type=F32)
        bcum_row = jnp.dot(jax.nn.log_sigmoid(g_row), triu, precision=lax.Precision.HIGHEST,
                           preferred_element_type=F32)

        for h in range(N_HEADS):
            i_col = g_col[:, h:h + 1]
            b_col = bcum_col[:, N_HEADS + h:N_HEADS + h + 1]
            i_row = g_row[h:h + 1, :]
            b_row = bcum_row[N_HEADS + h:N_HEADS + h + 1, :]
            b_last = b_row[:, CHUNK - 1:CHUNK]
            m_prev = m_s[h][0:1, 0:1]

            q_h = q_s[rows, h * DQK:(h + 1) * DQK]
            k_h = k_s[rows, h * DQK:(h + 1) * DQK]
            va_h = vaug_s[rows, h * DV_AUG:(h + 1) * DV_AUG]

            dmat = jnp.where(causal, b_col - b_row + i_row, -jnp.inf)
            m_intra = jnp.max(dmat, axis=-1, keepdims=True)
            m_inter = b_col + m_prev
            m_t = jnp.maximum(m_inter, m_intra)
            inter_scale = jnp.exp(m_inter - m_t)
            p = jnp.exp(dmat - m_t)
            s_qk = lax.dot_general(q_h, k_h.astype(BF16), (((1,), (1,)), ((), ())),
                                   preferred_element_type=F32) * p
            nd = (inter_scale * jnp.dot(q_h, c_s[h].astype(BF16), preferred_element_type=F32)
                  + jnp.dot(s_qk.astype(BF16), va_h, preferred_element_type=F32))
            num = nd[:, :DV]
            den = nd[:, DV:DV + 1]
            hc = num / jnp.maximum(jnp.abs(den), jnp.exp(-m_t))
            hn = _rms(hc) * gmh_ref[:, h * DV:(h + 1) * DV]
            og = jax.nn.sigmoid(o_ref[rows, h * DV:(h + 1) * DV].astype(F32))
            z_s[rows, h * DV:(h + 1) * DV] = (og * hn).astype(BF16)

            w_row = b_last - b_row + i_row
            m_loc = jnp.max(w_row, axis=-1, keepdims=True)
            e_col = jnp.exp(b_last - b_col + i_col - m_loc)
            c_inc = lax.dot_general((k_h * e_col).astype(BF16), va_h, (((0,), (0,)), ((), ())),
                                    preferred_element_type=F32)
            m_new = jnp.maximum(b_last + m_prev, m_loc)
            c_s[h] = jnp.exp(b_last + m_prev - m_new) * c_s[h] + jnp.exp(m_loc - m_new) * c_inc
            m_s[h] = jnp.broadcast_to(m_new, (SUBLANES, LANES))

        gg = jax.nn.gelu(gv_ref[rows, :].astype(F32))
        mu = jnp.mean(gg, axis=-1, keepdims=True)
        var = jnp.mean(jnp.square(gg - mu), axis=-1, keepdims=True)
        vln = ((gg - mu) * lax.rsqrt(var + EPS) * lng_ref[...] + lnb_ref[...]).astype(BF16)
        for g in range(N_GROUPS):
            cols = slice(g * GROUP, (g + 1) * GROUP)
            zg = jnp.dot(wc_s[g], vln[:, cols], preferred_element_type=F32) + bsp_ref[:, g:g + 1]
            gu = jax.nn.gelu(u_ref[rows, cols].astype(F32))
            z_s[rows, D_MLSTM + g * GROUP:D_MLSTM + (g + 1) * GROUP] = (gu * zg).astype(BF16)
        return carry

    lax.fori_loop(0, n_chunks, chunk_body, 0)

    y = jnp.dot(z_s[...], wout_ref[...], preferred_element_type=F32)
    out_ref[...] = x_ref[...] + gate_ref[...] * (_rms(y) * gpost_ref[...])


def _mixer_call(proj, g_col, g_row, x, gate, conv_w, conv_b, b_col, b_row, g_mh, ln_g, ln_b,
                w_sp, b_sp_t, w_out, g_post, bsz, seq, ts=512):
    t, d = x.shape
    nts = seq // ts
    nck = ts // CHUNK
    rowi = lambda k: (lambda b, s: (b * nts + s, k))
    const2 = lambda b, s: (0, 0)
    return pl.pallas_call(
        functools.partial(_mixer_kernel, ts=ts),
        grid=(bsz, nts),
        in_specs=[
            pl.BlockSpec((ts, 2 * D_QK), rowi(0)),
            pl.BlockSpec((ts, D_MLSTM), rowi(1)),
            pl.BlockSpec((ts, D_MLSTM), rowi(2)),
            pl.BlockSpec((ts, D_GMLP), rowi(3)),
            pl.BlockSpec((ts, D_GMLP), rowi(4)),
            pl.BlockSpec((ts, LANES), rowi(0)),
            pl.BlockSpec((None, nck, SUBLANES, CHUNK), lambda b, s: (b, s, 0, 0)),
            pl.BlockSpec((ts, d), rowi(0)),
            pl.BlockSpec((None, 1, d), lambda b, s: (b, 0, 0)),
            pl.BlockSpec((CONV_WIDTH, 2 * D_QK), const2),
            pl.BlockSpec((1, 2 * D_QK), const2),
            pl.BlockSpec((1, LANES), const2),
            pl.BlockSpec((SUBLANES, 1), const2),
            pl.BlockSpec((1, D_MLSTM), const2),
            pl.BlockSpec((1, D_GMLP), const2),
            pl.BlockSpec((1, D_GMLP), const2),
            pl.BlockSpec((N_GROUPS, CHUNK, CHUNK), lambda b, s: (0, 0, 0)),
            pl.BlockSpec((CHUNK, N_GROUPS), const2),
            pl.BlockSpec((d, d), const2),
            pl.BlockSpec((1, d), const2),
        ],
        out_specs=pl.BlockSpec((ts, d), rowi(0)),
        out_shape=jax.ShapeDtypeStruct((t, d), F32),
        scratch_shapes=[
            pltpu.VMEM((ts + 2 * SUBLANES, 2 * D_QK), F32),
            pltpu.VMEM((ts, D_QK), BF16),
            pltpu.VMEM((ts, D_QK), F32),
            pltpu.VMEM((ts, N_HEADS * DV_AUG), BF16),
            pltpu.VMEM((ts, d), BF16),
            pltpu.VMEM((N_GROUPS, CHUNK, CHUNK), BF16),
            pltpu.VMEM((N_HEADS, DQK, DV_AUG), F32),
            pltpu.VMEM((N_HEADS, SUBLANES, LANES), F32),
        ],
        compiler_params=pltpu.CompilerParams(
            dimension_semantics=("arbitrary", "arbitrary"), vmem_limit_bytes=VMEM_LIMIT),
        name="mixer_core",
    )(proj, proj, proj, proj, proj, g_col, g_row, x, gate, conv_w, conv_b, b_col, b_row,
      g_mh, ln_g, ln_b, w_sp, b_sp_t, w_out, g_post)


def kernel(x, c, w_ada, b_ada, g_pre, g_post, w_ff_gate, w_ff_up, w_ff_down, w_in, conv_w, conv_b,
           b_igate, b_fgate, g_mhnorm, gmlp_ln_g, gmlp_ln_b, w_spatial, b_spatial, w_out):
    bsz, seq, d = x.shape
    depth = w_ada.shape[0]
    xf = x.reshape(bsz * seq, d)
    for l in range(depth):
        mod = _ada_call(c, w_ada[l], b_ada[l]).reshape(bsz, N_SUB, N_MOD, 1, d)
        shift = lambda j: mod[:, j, 0]
        scale = lambda j: mod[:, j, 1]
        gate = lambda j: mod[:, j, 2]
        gp = lambda j: g_pre[l, j].reshape(1, d)
        gq = lambda j: g_post[l, j].reshape(1, d)

        def ffn(xin, j, idx):
            return _ffn_call(xin, shift(j), scale(j), gate(j), gp(j), gq(j),
                             w_ff_gate[l, idx].astype(BF16), w_ff_up[l, idx].astype(BF16),
                             w_ff_down[l, idx].astype(BF16), 0.5, seq)

        xf = ffn(xf, 0, 0)

        w_main = jnp.concatenate([w_in[l][:, :GATE_LO], w_in[l][:, GATE_HI:]], axis=1).astype(BF16)
        w_gate = jnp.pad(w_in[l][:, GATE_LO:GATE_HI], ((0, 0), (0, LANES - 2 * N_HEADS))).astype(BF16)
        proj, g_col = _inproj_call(xf, shift(1), scale(1), gp(1), w_main, w_gate, seq)
        g_row = (g_col[:, :SUBLANES].reshape(bsz, seq // CHUNK, CHUNK, SUBLANES)
                 .transpose(0, 1, 3, 2))
        bias = jnp.concatenate([b_igate[l], b_fgate[l]])
        b_col = jnp.pad(bias, (0, LANES - 2 * N_HEADS)).reshape(1, LANES)
        b_row = bias.reshape(SUBLANES, 1)
        xf = _mixer_call(proj, g_col, g_row, xf, gate(1), conv_w[l], conv_b[l].reshape(1, -1),
                         b_col, b_row, g_mhnorm[l].reshape(1, -1), gmlp_ln_g[l].reshape(1, -1),
                         gmlp_ln_b[l].reshape(1, -1), w_spatial[l], b_spatial[l].T,
                         w_out[l].astype(BF16), gq(1), bsz, seq)

        xf = ffn(xf, 2, 1)
    return xf.reshape(bsz, seq, d)
```

```python
import functools

import jax
import jax.numpy as jnp
from jax import lax
from jax.experimental import pallas as pl
from jax.experimental.pallas import tpu as pltpu

D_MODEL = 2048
D_MLSTM = D_MODEL // 2
N_HEADS = 4
DV = D_MLSTM // N_HEADS
DQK = DV // 2
D_QK = N_HEADS * DQK
CHUNK = 128
CONV_WIDTH = 4
D_GMLP = D_MODEL - D_MLSTM
N_GROUPS = 8
GROUP = D_GMLP // N_GROUPS
D_FF = 5632
N_SUB = 3
N_MOD = 3
EPS = 1e-6
GATE_LO = 2 * D_QK + 2 * D_MLSTM
GATE_HI = GATE_LO + 2 * N_HEADS
D_MAIN = 2 * D_QK + 2 * D_MLSTM + 2 * D_GMLP
LANES = 128
SUBLANES = 8
DV_AUG = DV + LANES

VMEM_LIMIT = 56 * 1024 * 1024

F32 = jnp.float32
BF16 = jnp.bfloat16


def _rms(y):
    return y * lax.rsqrt(jnp.mean(y * y, axis=-1, keepdims=True) + EPS)


ROW_BLOCK = 16
NORM_UNROLL = 4


def _modulated_norm_rows(x_ref, gpre_ref, scale_ref, shift_ref, h_ref):
    mult = gpre_ref[...] * (1.0 + scale_ref[...])
    shift = shift_ref[...]

    def body(r, carry):
        rows = pl.ds(pl.multiple_of(r * ROW_BLOCK, ROW_BLOCK), ROW_BLOCK)
        h_ref[rows, :] = (_rms(x_ref[rows, :]) * mult + shift).astype(h_ref.dtype)
        return carry

    lax.fori_loop(0, x_ref.shape[0] // ROW_BLOCK, body, 0, unroll=NORM_UNROLL)


def _gated_residual_rows(x_ref, y_ref, gpost_ref, gate_ref, coef, o_ref):
    mult = gpost_ref[...] * (coef * gate_ref[...])

    def body(r, carry):
        rows = pl.ds(pl.multiple_of(r * ROW_BLOCK, ROW_BLOCK), ROW_BLOCK)
        o_ref[rows, :] = x_ref[rows, :] + _rms(y_ref[rows, :]) * mult
        return carry

    lax.fori_loop(0, x_ref.shape[0] // ROW_BLOCK, body, 0, unroll=NORM_UNROLL)


def _ada_kernel(c_ref, w_ref, b_ref, o_ref):
    a = jax.nn.silu(c_ref[...]).astype(BF16)
    o_ref[...] = jnp.dot(a, w_ref[...].astype(BF16), preferred_element_type=F32) + b_ref[...]


def _ada_call(c, w, b, tn=1024):
    bsz, d = c.shape
    n = w.shape[1]
    return pl.pallas_call(
        _ada_kernel,
        grid=(n // tn,),
        in_specs=[
            pl.BlockSpec((bsz, d), lambda j: (0, 0)),
            pl.BlockSpec((d, tn), lambda j: (0, j)),
            pl.BlockSpec((1, tn), lambda j: (0, j)),
        ],
        out_specs=pl.BlockSpec((bsz, tn), lambda j: (0, j)),
        out_shape=jax.ShapeDtypeStruct((bsz, n), F32),
        compiler_params=pltpu.CompilerParams(
            dimension_semantics=("arbitrary",), vmem_limit_bytes=VMEM_LIMIT),
        name="adaln_mod",
    )(c, w, b.reshape(1, n))


def _ffn_kernel(x_ref, shift_ref, scale_ref, gate_ref, gpre_ref, gpost_ref,
                wg_ref, wu_ref, wd_ref, o_ref, h_ref, acc_ref, *, coef):
    j = pl.program_id(1)

    @pl.when(j == 0)
    def _():
        _modulated_norm_rows(x_ref, gpre_ref, scale_ref, shift_ref, h_ref)

    h = h_ref[...]
    g = jnp.dot(h, wg_ref[...], preferred_element_type=F32)
    u = jnp.dot(h, wu_ref[...], preferred_element_type=F32)
    a = (jax.nn.silu(g) * u).astype(BF16)
    acc_ref[...] = (jnp.where(j > 0, acc_ref[...], 0.0)
                    + jnp.dot(a, wd_ref[...], preferred_element_type=F32))

    @pl.when(j == pl.num_programs(1) - 1)
    def _():
        _gated_residual_rows(x_ref, acc_ref, gpost_ref, gate_ref, coef, o_ref)


def _ffn_call(x, shift, scale, gate, g_pre, g_post, wg, wu, wd, coef, seq, tm=512, tf=512):
    t, d = x.shape
    ff = wg.shape[1]
    per_b = seq // tm
    row = lambda i, j: (i, 0)
    modm = lambda i, j: (i // per_b, 0, 0)
    const = lambda i, j: (0, 0)
    return pl.pallas_call(
        functools.partial(_ffn_kernel, coef=coef),
        grid=(t // tm, ff // tf),
        in_specs=[
            pl.BlockSpec((tm, d), row),
            pl.BlockSpec((None, 1, d), modm),
            pl.BlockSpec((None, 1, d), modm),
            pl.BlockSpec((None, 1, d), modm),
            pl.BlockSpec((1, d), const),
            pl.BlockSpec((1, d), const),
            pl.BlockSpec((d, tf), lambda i, j: (0, j)),
            pl.BlockSpec((d, tf), lambda i, j: (0, j)),
            pl.BlockSpec((tf, d), lambda i, j: (j, 0)),
        ],
        out_specs=pl.BlockSpec((tm, d), row),
        out_shape=jax.ShapeDtypeStruct((t, d), F32),
        scratch_shapes=[pltpu.VMEM((tm, d), BF16), pltpu.VMEM((tm, d), F32)],
        compiler_params=pltpu.CompilerParams(
            dimension_semantics=("parallel", "arbitrary"), vmem_limit_bytes=VMEM_LIMIT),
        name="ffn_sublayer",
    )(x, shift, scale, gate, g_pre, g_post, wg, wu, wd)


def _inproj_kernel(x_ref, shift_ref, scale_ref, gpre_ref, w_ref, wgate_ref,
                   o_ref, ogate_ref, h_ref):
    j = pl.program_id(1)

    @pl.when(j == 0)
    def _():
        _modulated_norm_rows(x_ref, gpre_ref, scale_ref, shift_ref, h_ref)
        ogate_ref[...] = jnp.dot(h_ref[...], wgate_ref[...], preferred_element_type=F32)

    o_ref[...] = jnp.dot(h_ref[...], w_ref[...], preferred_element_type=F32).astype(BF16)


def _inproj_call(x, shift, scale, g_pre, w_main, w_gate, seq, tm=1024, tn=1024):
    t, d = x.shape
    n = w_main.shape[1]
    per_b = seq // tm
    modm = lambda i, j: (i // per_b, 0, 0)
    return pl.pallas_call(
        _inproj_kernel,
        grid=(t // tm, n // tn),
        in_specs=[
            pl.BlockSpec((tm, d), lambda i, j: (i, 0)),
            pl.BlockSpec((None, 1, d), modm),
            pl.BlockSpec((None, 1, d), modm),
            pl.BlockSpec((1, d), lambda i, j: (0, 0)),
            pl.BlockSpec((d, tn), lambda i, j: (0, j)),
            pl.BlockSpec((d, LANES), lambda i, j: (0, 0)),
        ],
        out_specs=[
            pl.BlockSpec((tm, tn), lambda i, j: (i, j)),
            pl.BlockSpec((tm, LANES), lambda i, j: (i, 0)),
        ],
        out_shape=[jax.ShapeDtypeStruct((t, n), BF16), jax.ShapeDtypeStruct((t, LANES), F32)],
        scratch_shapes=[pltpu.VMEM((tm, d), BF16)],
        compiler_params=pltpu.CompilerParams(
            dimension_semantics=("parallel", "arbitrary"), vmem_limit_bytes=VMEM_LIMIT),
        name="mixer_inproj",
    )(x, shift, scale, g_pre, w_main, w_gate)


def _mixer_kernel(qk_ref, v_ref, o_ref, u_ref, gv_ref, gcol_ref, grow_ref, x_ref, gate_ref,
                  convw_ref, convb_ref, bcol_ref, brow_ref, gmh_ref, lng_ref, lnb_ref,
                  wsp_ref, bsp_ref, wout_ref, gpost_ref,
                  out_ref,
                  qkbuf, q_s, k_s, vaug_s, z_s, wc_s, c_s, m_s, *, ts):
    s_idx = pl.program_id(1)
    n_chunks = ts // CHUNK
    tail = SUBLANES

    @pl.when(s_idx == 0)
    def _():
        qkbuf[0:tail, :] = jnp.zeros((tail, 2 * D_QK), F32)
        c_s[...] = jnp.zeros_like(c_s)
        m_s[...] = jnp.zeros_like(m_s)
        row = lax.broadcasted_iota(jnp.int32, (CHUNK, CHUNK), 0)
        col = lax.broadcasted_iota(jnp.int32, (CHUNK, CHUNK), 1)
        for g in range(N_GROUPS):
            wc_s[g] = jnp.where(col <= row, wsp_ref[g], 0.0).astype(BF16)

    qkbuf[tail:tail + ts, :] = qk_ref[...].astype(F32)
    for c in range(n_chunks):
        base = tail + c * CHUNK - (CONV_WIDTH - 1)
        acc = convb_ref[...] + convw_ref[0:1, :] * qkbuf[base:base + CHUNK, :]
        for j in range(1, CONV_WIDTH):
            acc = acc + convw_ref[j:j + 1, :] * qkbuf[base + j:base + j + CHUNK, :]
        act = jax.nn.silu(acc)
        q_s[c * CHUNK:(c + 1) * CHUNK, :] = act[:, :D_QK].astype(BF16)
        k_s[c * CHUNK:(c + 1) * CHUNK, :] = act[:, D_QK:] * (DQK ** -0.5)
    qkbuf[0:tail, :] = qkbuf[ts:ts + tail, :]

    for h in range(N_HEADS):
        vaug_s[:, h * DV_AUG:h * DV_AUG + DV] = v_ref[:, h * DV:(h + 1) * DV]
        vaug_s[:, h * DV_AUG + DV:(h + 1) * DV_AUG] = jnp.ones((ts, LANES), BF16)

    row = lax.broadcasted_iota(jnp.int32, (CHUNK, CHUNK), 0)
    col = lax.broadcasted_iota(jnp.int32, (CHUNK, CHUNK), 1)
    causal = col <= row
    tril = jnp.where(causal, 1.0, 0.0).astype(F32)
    triu = jnp.where(row <= col, 1.0, 0.0).astype(F32)

    def chunk_body(c, carry):
        r0 = pl.multiple_of(c * CHUNK, CHUNK)
        rows = pl.ds(r0, CHUNK)

        g_col = gcol_ref[rows, :] + bcol_ref[...]
        g_row = grow_ref[c] + brow_ref[...]
        bcum_col = jnp.dot(tril, jax.nn.log_sigmoid(g_col), precision=lax.Precision.HIGHEST,
                           preferred_element_type=F32)
        bcum_row = jnp.dot(jax.nn.log_sigmoid(g_row), triu, precision=lax.Precision.HIGHEST,
                           preferred_element_type=F32)

        for h in range(N_HEADS):
            i_col = g_col[:, h:h + 1]
            b_col = bcum_col[:, N_HEADS + h:N_HEADS + h + 1]
            i_row = g_row[h:h + 1, :]
            b_row = bcum_row[N_HEADS + h:N_HEADS + h + 1, :]
            b_last = b_row[:, CHUNK - 1:CHUNK]
            m_prev = m_s[h][0:1, 0:1]

            q_h = q_s[rows, h * DQK:(h + 1) * DQK]
            k_h = k_s[rows, h * DQK:(h + 1) * DQK]
            va_h = vaug_s[rows, h * DV_AUG:(h + 1) * DV_AUG]

            dmat = jnp.where(causal, b_col - b_row + i_row, -jnp.inf)
            m_intra = jnp.max(dmat, axis=-1, keepdims=True)
            m_inter = b_col + m_prev
            m_t = jnp.maximum(m_inter, m_intra)
            inter_scale = jnp.exp(m_inter - m_t)
            p = jnp.exp(dmat - m_t)
            s_qk = lax.dot_general(q_h, k_h.astype(BF16), (((1,), (1,)), ((), ())),
                                   preferred_element_type=F32) * p
            nd = (inter_scale * jnp.dot(q_h, c_s[h].astype(BF16), preferred_element_type=F32)
                  + jnp.dot(s_qk.astype(BF16), va_h, preferred_element_type=F32))
            num = nd[:, :DV]
            den = nd[:, DV:DV + 1]
            hc = num / jnp.maximum(jnp.abs(den), jnp.exp(-m_t))
            hn = _rms(hc) * gmh_ref[:, h * DV:(h + 1) * DV]
            og = jax.nn.sigmoid(o_ref[rows, h * DV:(h + 1) * DV].astype(F32))
            z_s[rows, h * DV:(h + 1) * DV] = (og * hn).astype(BF16)

            w_row = b_last - b_row + i_row
            m_loc = jnp.max(w_row, axis=-1, keepdims=True)
            e_col = jnp.exp(b_last - b_col + i_col - m_loc)
            c_inc = lax.dot_general((k_h * e_col).astype(BF16), va_h, (((0,), (0,)), ((), ())),
                                    preferred_element_type=F32)
            m_new = jnp.maximum(b_last + m_prev, m_loc)
            c_s[h] = jnp.exp(b_last + m_prev - m_new) * c_s[h] + jnp.exp(m_loc - m_new) * c_inc
            m_s[h] = jnp.broadcast_to(m_new, (SUBLANES, LANES))

        gg = jax.nn.gelu(gv_ref[rows, :].astype(F32))
        mu = jnp.mean(gg, axis=-1, keepdims=True)
        var = jnp.mean(jnp.square(gg - mu), axis=-1, keepdims=True)
        vln = ((gg - mu) * lax.rsqrt(var + EPS) * lng_ref[...] + lnb_ref[...]).astype(BF16)
        for g in range(N_GROUPS):
            cols = slice(g * GROUP, (g + 1) * GROUP)
            zg = jnp.dot(wc_s[g], vln[:, cols], preferred_element_type=F32) + bsp_ref[:, g:g + 1]
            gu = jax.nn.gelu(u_ref[rows, cols].astype(F32))
            z_s[rows, D_MLSTM + g * GROUP:D_MLSTM + (g + 1) * GROUP] = (gu * zg).astype(BF16)
        return carry

    lax.fori_loop(0, n_chunks, chunk_body, 0)

    out_ref[...] = jnp.dot(z_s[...], wout_ref[...], preferred_element_type=F32)
    _gated_residual_rows(x_ref, out_ref, gpost_ref, gate_ref, 1.0, out_ref)


def _mixer_call(proj, g_col, g_row, x, gate, conv_w, conv_b, b_col, b_row, g_mh, ln_g, ln_b,
                w_sp, b_sp_t, w_out, g_post, bsz, seq, ts=512):
    t, d = x.shape
    nts = seq // ts
    nck = ts // CHUNK
    rowi = lambda k: (lambda b, s: (b * nts + s, k))
    const2 = lambda b, s: (0, 0)
    return pl.pallas_call(
        functools.partial(_mixer_kernel, ts=ts),
        grid=(bsz, nts),
        in_specs=[
            pl.BlockSpec((ts, 2 * D_QK), rowi(0)),
            pl.BlockSpec((ts, D_MLSTM), rowi(1)),
            pl.BlockSpec((ts, D_MLSTM), rowi(2)),
            pl.BlockSpec((ts, D_GMLP), rowi(3)),
            pl.BlockSpec((ts, D_GMLP), rowi(4)),
            pl.BlockSpec((ts, LANES), rowi(0)),
            pl.BlockSpec((None, nck, SUBLANES, CHUNK), lambda b, s: (b, s, 0, 0)),
            pl.BlockSpec((ts, d), rowi(0)),
            pl.BlockSpec((None, 1, d), lambda b, s: (b, 0, 0)),
            pl.BlockSpec((CONV_WIDTH, 2 * D_QK), const2),
            pl.BlockSpec((1, 2 * D_QK), const2),
            pl.BlockSpec((1, LANES), const2),
            pl.BlockSpec((SUBLANES, 1), const2),
            pl.BlockSpec((1, D_MLSTM), const2),
            pl.BlockSpec((1, D_GMLP), const2),
            pl.BlockSpec((1, D_GMLP), const2),
            pl.BlockSpec((N_GROUPS, CHUNK, CHUNK), lambda b, s: (0, 0, 0)),
            pl.BlockSpec((CHUNK, N_GROUPS), const2),
            pl.BlockSpec((d, d), const2),
            pl.BlockSpec((1, d), const2),
        ],
        out_specs=pl.BlockSpec((ts, d), rowi(0)),
        out_shape=jax.ShapeDtypeStruct((t, d), F32),
        scratch_shapes=[
            pltpu.VMEM((ts + 2 * SUBLANES, 2 * D_QK), F32),
            pltpu.VMEM((ts, D_QK), BF16),
            pltpu.VMEM((ts, D_QK), F32),
            pltpu.VMEM((ts, N_HEADS * DV_AUG), BF16),
            pltpu.VMEM((ts, d), BF16),
            pltpu.VMEM((N_GROUPS, CHUNK, CHUNK), BF16),
            pltpu.VMEM((N_HEADS, DQK, DV_AUG), F32),
            pltpu.VMEM((N_HEADS, SUBLANES, LANES), F32),
        ],
        compiler_params=pltpu.CompilerParams(
            dimension_semantics=("arbitrary", "arbitrary"), vmem_limit_bytes=VMEM_LIMIT),
        name="mixer_core",
    )(proj, proj, proj, proj, proj, g_col, g_row, x, gate, conv_w, conv_b, b_col, b_row,
      g_mh, ln_g, ln_b, w_sp, b_sp_t, w_out, g_post)


def kernel(x, c, w_ada, b_ada, g_pre, g_post, w_ff_gate, w_ff_up, w_ff_down, w_in, conv_w, conv_b,
           b_igate, b_fgate, g_mhnorm, gmlp_ln_g, gmlp_ln_b, w_spatial, b_spatial, w_out):
    bsz, seq, d = x.shape
    depth = w_ada.shape[0]
    xf = x.reshape(bsz * seq, d)
    for l in range(depth):
        mod = _ada_call(c, w_ada[l], b_ada[l]).reshape(bsz, N_SUB, N_MOD, 1, d)
        shift = lambda j: mod[:, j, 0]
        scale = lambda j: mod[:, j, 1]
        gate = lambda j: mod[:, j, 2]
        gp = lambda j: g_pre[l, j].reshape(1, d)
        gq = lambda j: g_post[l, j].reshape(1, d)

        def ffn(xin, j, idx):
            return _ffn_call(xin, shift(j), scale(j), gate(j), gp(j), gq(j),
                             w_ff_gate[l, idx].astype(BF16), w_ff_up[l, idx].astype(BF16),
                             w_ff_down[l, idx].astype(BF16), 0.5, seq)

        xf = ffn(xf, 0, 0)

        w_main = jnp.concatenate([w_in[l][:, :GATE_LO], w_in[l][:, GATE_HI:]], axis=1).astype(BF16)
        w_gate = jnp.pad(w_in[l][:, GATE_LO:GATE_HI], ((0, 0), (0, LANES - 2 * N_HEADS))).astype(BF16)
        proj, g_col = _inproj_call(xf, shift(1), scale(1), gp(1), w_main, w_gate, seq)
        g_row = (g_col[:, :SUBLANES].reshape(bsz, seq // CHUNK, CHUNK, SUBLANES)
                 .transpose(0, 1, 3, 2))
        bias = jnp.concatenate([b_igate[l], b_fgate[l]])
        b_col = jnp.pad(bias, (0, LANES - 2 * N_HEADS)).reshape(1, LANES)
        b_row = bias.reshape(SUBLANES, 1)
        xf = _mixer_call(proj, g_col, g_row, xf, gate(1), conv_w[l], conv_b[l].reshape(1, -1),
                         b_col, b_row, g_mhnorm[l].reshape(1, -1), gmlp_ln_g[l].reshape(1, -1),
                         gmlp_ln_b[l].reshape(1, -1), w_spatial[l], b_spatial[l].T,
                         w_out[l].astype(BF16), gq(1), bsz, seq)

        xf = ffn(xf, 2, 1)
    return xf.reshape(bsz, seq, d)
```

```python
import functools

import jax
import jax.numpy as jnp
from jax import lax
from jax.experimental import pallas as pl
from jax.experimental.pallas import tpu as pltpu

D_MODEL = 2048
D_MLSTM = D_MODEL // 2
N_HEADS = 4
DV = D_MLSTM // N_HEADS
DQK = DV // 2
D_QK = N_HEADS * DQK
CHUNK = 128
CONV_WIDTH = 4
D_GMLP = D_MODEL - D_MLSTM
N_GROUPS = 8
GROUP = D_GMLP // N_GROUPS
D_FF = 5632
N_SUB = 3
N_MOD = 3
EPS = 1e-6
GATE_LO = 2 * D_QK + 2 * D_MLSTM
GATE_HI = GATE_LO + 2 * N_HEADS
D_MAIN = 2 * D_QK + 2 * D_MLSTM + 2 * D_GMLP
LANES = 128
SUBLANES = 8
DV_AUG = DV + LANES

VMEM_LIMIT = 56 * 1024 * 1024

F32 = jnp.float32
BF16 = jnp.bfloat16


def _rms(y):
    return y * lax.rsqrt(jnp.mean(y * y, axis=-1, keepdims=True) + EPS)


ROW_BLOCK = 16
NORM_UNROLL = 4


def _modulated_norm_rows(x_ref, gpre_ref, scale_ref, shift_ref, h_ref):
    mult = gpre_ref[...] * (1.0 + scale_ref[...])
    shift = shift_ref[...]

    def body(r, carry):
        rows = pl.ds(pl.multiple_of(r * ROW_BLOCK, ROW_BLOCK), ROW_BLOCK)
        h_ref[rows, :] = (_rms(x_ref[rows, :]) * mult + shift).astype(h_ref.dtype)
        return carry

    lax.fori_loop(0, x_ref.shape[0] // ROW_BLOCK, body, 0, unroll=NORM_UNROLL)


def _gated_residual_rows(x_ref, y_ref, gpost_ref, gate_ref, coef, o_ref):
    mult = gpost_ref[...] * (coef * gate_ref[...])

    def body(r, carry):
        rows = pl.ds(pl.multiple_of(r * ROW_BLOCK, ROW_BLOCK), ROW_BLOCK)
        o_ref[rows, :] = x_ref[rows, :] + _rms(y_ref[rows, :]) * mult
        return carry

    lax.fori_loop(0, x_ref.shape[0] // ROW_BLOCK, body, 0, unroll=NORM_UNROLL)


def _ada_kernel(c_ref, w_ref, b_ref, o_ref):
    a = jax.nn.silu(c_ref[...]).astype(BF16)
    o_ref[...] = jnp.dot(a, w_ref[...].astype(BF16), preferred_element_type=F32) + b_ref[...]


def _ada_call(c, w, b, tn=1024):
    bsz, d = c.shape
    n = w.shape[1]
    return pl.pallas_call(
        _ada_kernel,
        grid=(n // tn,),
        in_specs=[
            pl.BlockSpec((bsz, d), lambda j: (0, 0)),
            pl.BlockSpec((d, tn), lambda j: (0, j)),
            pl.BlockSpec((1, tn), lambda j: (0, j)),
        ],
        out_specs=pl.BlockSpec((bsz, tn), lambda j: (0, j)),
        out_shape=jax.ShapeDtypeStruct((bsz, n), F32),
        compiler_params=pltpu.CompilerParams(
            dimension_semantics=("arbitrary",), vmem_limit_bytes=VMEM_LIMIT),
        name="adaln_mod",
    )(c, w, b.reshape(1, n))


SIDE_ROWS = 48


def _ffn_kernel(xn_ref, xp_ref, shiftn_ref, scalen_ref, shiftp_ref, scalep_ref, gatep_ref,
                gpre_ref, gpost_ref, wg_ref, wu_ref, wd_ref, o_ref,
                h0_ref, h1_ref, acc0_ref, acc1_ref, *, coef, n_tiles):
    i = pl.program_id(0)
    j = pl.program_id(1)
    tm = xn_ref.shape[0]
    h_refs = (h0_ref, h1_ref)
    acc_refs = (acc0_ref, acc1_ref)

    @pl.when((i == 0) & (j == 0))
    def _():
        _modulated_norm_rows(xp_ref, gpre_ref, scalep_ref, shiftp_ref, h0_ref)
        acc0_ref[...] = jnp.zeros_like(acc0_ref)
        acc1_ref[...] = jnp.zeros_like(acc1_ref)

    def tile_step(cur):
        h_cur, h_nxt = h_refs[cur], h_refs[1 - cur]
        acc_cur, acc_nxt = acc_refs[cur], acc_refs[1 - cur]

        start = jnp.minimum(j * SIDE_ROWS, tm - SIDE_ROWS)
        pre_mult = gpre_ref[...] * (1.0 + scalen_ref[...])
        pre_shift = shiftn_ref[...]
        post_mult = gpost_ref[...] * (coef * gatep_ref[...])
        for k in range(SIDE_ROWS // ROW_BLOCK):
            rows = pl.ds(pl.multiple_of(start + k * ROW_BLOCK, ROW_BLOCK), ROW_BLOCK)
            h_nxt[rows, :] = (_rms(xn_ref[rows, :]) * pre_mult + pre_shift).astype(BF16)
            o_ref[rows, :] = xp_ref[rows, :] + _rms(acc_nxt[rows, :]) * post_mult

        h = h_cur[...]
        g = jnp.dot(h, wg_ref[...], preferred_element_type=F32)
        u = jnp.dot(h, wu_ref[...], preferred_element_type=F32)
        a = (jax.nn.silu(g) * u).astype(BF16)
        acc_cur[...] = (jnp.where(j > 0, acc_cur[...], 0.0)
                        + jnp.dot(a, wd_ref[...], preferred_element_type=F32))

    for parity in range(2):
        pl.when((i < n_tiles) & (i % 2 == parity))(functools.partial(tile_step, parity))

    @pl.when((i == n_tiles) & (j == 0))
    def _():
        _gated_residual_rows(xp_ref, acc_refs[(n_tiles - 1) % 2], gpost_ref, gatep_ref, coef, o_ref)


def _ffn_call(x, shift, scale, gate, g_pre, g_post, wg, wu, wd, coef, seq, tm=512, tf=512):
    t, d = x.shape
    ff = wg.shape[1]
    per_b = seq // tm
    n_tiles = t // tm
    n_ff = ff // tf
    assert (n_ff - 1) * SIDE_ROWS < tm <= n_ff * SIDE_ROWS and (tm - SIDE_ROWS) % ROW_BLOCK == 0
    nxt_tile = lambda i: jnp.minimum(i + 1, n_tiles - 1)
    prv_tile = lambda i: jnp.maximum(i - 1, 0)
    ff_tile = lambda i, j: jnp.where(i < n_tiles, j, n_ff - 1)
    const = lambda i, j: (0, 0)
    mod_n = pl.BlockSpec((None, 1, d), lambda i, j: (nxt_tile(i) // per_b, 0, 0))
    mod_p = pl.BlockSpec((None, 1, d), lambda i, j: (prv_tile(i) // per_b, 0, 0))
    return pl.pallas_call(
        functools.partial(_ffn_kernel, coef=coef, n_tiles=n_tiles),
        grid=(n_tiles + 1, n_ff),
        in_specs=[
            pl.BlockSpec((tm, d), lambda i, j: (nxt_tile(i), 0)),
            pl.BlockSpec((tm, d), lambda i, j: (prv_tile(i), 0)),
            mod_n, mod_n, mod_p, mod_p, mod_p,
            pl.BlockSpec((1, d), const),
            pl.BlockSpec((1, d), const),
            pl.BlockSpec((d, tf), lambda i, j: (0, ff_tile(i, j))),
            pl.BlockSpec((d, tf), lambda i, j: (0, ff_tile(i, j))),
            pl.BlockSpec((tf, d), lambda i, j: (ff_tile(i, j), 0)),
        ],
        out_specs=pl.BlockSpec((tm, d), lambda i, j: (prv_tile(i), 0)),
        out_shape=jax.ShapeDtypeStruct((t, d), F32),
        scratch_shapes=[pltpu.VMEM((tm, d), BF16), pltpu.VMEM((tm, d), BF16),
                        pltpu.VMEM((tm, d), F32), pltpu.VMEM((tm, d), F32)],
        compiler_params=pltpu.CompilerParams(
            dimension_semantics=("arbitrary", "arbitrary"), vmem_limit_bytes=VMEM_LIMIT),
        name="ffn_sublayer",
    )(x, x, shift, scale, shift, scale, gate, g_pre, g_post, wg, wu, wd)


def _inproj_kernel(x_ref, shift_ref, scale_ref, gpre_ref, w_ref, wgate_ref,
                   o_ref, ogate_ref, h_ref):
    j = pl.program_id(1)

    @pl.when(j == 0)
    def _():
        _modulated_norm_rows(x_ref, gpre_ref, scale_ref, shift_ref, h_ref)
        ogate_ref[...] = jnp.dot(h_ref[...], wgate_ref[...], preferred_element_type=F32)

    o_ref[...] = jnp.dot(h_ref[...], w_ref[...], preferred_element_type=F32).astype(BF16)


def _inproj_call(x, shift, scale, g_pre, w_main, w_gate, seq, tm=1024, tn=1024):
    t, d = x.shape
    n = w_main.shape[1]
    per_b = seq // tm
    modm = lambda i, j: (i // per_b, 0, 0)
    return pl.pallas_call(
        _inproj_kernel,
        grid=(t // tm, n // tn),
        in_specs=[
            pl.BlockSpec((tm, d), lambda i, j: (i, 0)),
            pl.BlockSpec((None, 1, d), modm),
            pl.BlockSpec((None, 1, d), modm),
            pl.BlockSpec((1, d), lambda i, j: (0, 0)),
            pl.BlockSpec((d, tn), lambda i, j: (0, j)),
            pl.BlockSpec((d, LANES), lambda i, j: (0, 0)),
        ],
        out_specs=[
            pl.BlockSpec((tm, tn), lambda i, j: (i, j)),
            pl.BlockSpec((tm, LANES), lambda i, j: (i, 0)),
        ],
        out_shape=[jax.ShapeDtypeStruct((t, n), BF16), jax.ShapeDtypeStruct((t, LANES), F32)],
        scratch_shapes=[pltpu.VMEM((tm, d), BF16)],
        compiler_params=pltpu.CompilerParams(
            dimension_semantics=("parallel", "arbitrary"), vmem_limit_bytes=VMEM_LIMIT),
        name="mixer_inproj",
    )(x, shift, scale, g_pre, w_main, w_gate)


def _mixer_kernel(qk_ref, v_ref, o_ref, u_ref, gv_ref, gcol_ref, grow_ref, x_ref, gate_ref,
                  convw_ref, convb_ref, bcol_ref, brow_ref, gmh_ref, lng_ref, lnb_ref,
                  wsp_ref, bsp_ref, wout_ref, gpost_ref,
                  out_ref,
                  qkbuf, q_s, k_s, vaug_s, z_s, y_s, wc_s, c_s, m_s, *, ts):
    s_idx = pl.program_id(1)
    n_chunks = ts // CHUNK
    tail = SUBLANES

    @pl.when(s_idx == 0)
    def _():
        qkbuf[0:tail, :] = jnp.zeros((tail, 2 * D_QK), F32)
        c_s[...] = jnp.zeros_like(c_s)
        m_s[...] = jnp.zeros_like(m_s)
        row = lax.broadcasted_iota(jnp.int32, (CHUNK, CHUNK), 0)
        col = lax.broadcasted_iota(jnp.int32, (CHUNK, CHUNK), 1)
        for g in range(N_GROUPS):
            wc_s[g] = jnp.where(col <= row, wsp_ref[g], 0.0).astype(BF16)

    qkbuf[tail:tail + ts, :] = qk_ref[...].astype(F32)
    for c in range(n_chunks):
        base = tail + c * CHUNK - (CONV_WIDTH - 1)
        acc = convb_ref[...] + convw_ref[0:1, :] * qkbuf[base:base + CHUNK, :]
        for j in range(1, CONV_WIDTH):
            acc = acc + convw_ref[j:j + 1, :] * qkbuf[base + j:base + j + CHUNK, :]
        act = jax.nn.silu(acc)
        q_s[c * CHUNK:(c + 1) * CHUNK, :] = act[:, :D_QK].astype(BF16)
        k_s[c * CHUNK:(c + 1) * CHUNK, :] = act[:, D_QK:] * (DQK ** -0.5)
    qkbuf[0:tail, :] = qkbuf[ts:ts + tail, :]

    for h in range(N_HEADS):
        vaug_s[:, h * DV_AUG:h * DV_AUG + DV] = v_ref[:, h * DV:(h + 1) * DV]
        vaug_s[:, h * DV_AUG + DV:(h + 1) * DV_AUG] = jnp.ones((ts, LANES), BF16)

    row = lax.broadcasted_iota(jnp.int32, (CHUNK, CHUNK), 0)
    col = lax.broadcasted_iota(jnp.int32, (CHUNK, CHUNK), 1)
    causal = col <= row
    tril = jnp.where(causal, 1.0, 0.0).astype(F32)
    triu = jnp.where(row <= col, 1.0, 0.0).astype(F32)

    def chunk_body(c, carry):
        r0 = pl.multiple_of(c * CHUNK, CHUNK)
        rows = pl.ds(r0, CHUNK)

        g_col = gcol_ref[rows, :] + bcol_ref[...]
        g_row = grow_ref[c] + brow_ref[...]
        bcum_col = jnp.dot(tril, jax.nn.log_sigmoid(g_col), precision=lax.Precision.HIGHEST,
                           preferred_element_type=F32)
        bcum_row = jnp.dot(jax.nn.log_sigmoid(g_row), triu, precision=lax.Precision.HIGHEST,
                           preferred_element_type=F32)

        for h in range(N_HEADS):
            i_col = g_col[:, h:h + 1]
            b_col = bcum_col[:, N_HEADS + h:N_HEADS + h + 1]
            i_row = g_row[h:h + 1, :]
            b_row = bcum_row[N_HEADS + h:N_HEADS + h + 1, :]
            b_last = b_row[:, CHUNK - 1:CHUNK]
            m_prev = m_s[h][0:1, 0:1]

            q_h = q_s[rows, h * DQK:(h + 1) * DQK]
            k_h = k_s[rows, h * DQK:(h + 1) * DQK]
            va_h = vaug_s[rows, h * DV_AUG:(h + 1) * DV_AUG]

            dmat = jnp.where(causal, b_col - b_row + i_row, -jnp.inf)
            m_intra = jnp.max(dmat, axis=-1, keepdims=True)
            m_inter = b_col + m_prev
            m_t = jnp.maximum(m_inter, m_intra)
            inter_scale = jnp.exp(m_inter - m_t)
            p = jnp.exp(dmat - m_t)
            s_qk = lax.dot_general(q_h, k_h.astype(BF16), (((1,), (1,)), ((), ())),
                                   preferred_element_type=F32) * p
            nd = (inter_scale * jnp.dot(q_h, c_s[h].astype(BF16), preferred_element_type=F32)
                  + jnp.dot(s_qk.astype(BF16), va_h, preferred_element_type=F32))
            num = nd[:, :DV]
            den = nd[:, DV:DV + 1]
            hc = num / jnp.maximum(jnp.abs(den), jnp.exp(-m_t))
            hn = _rms(hc) * gmh_ref[:, h * DV:(h + 1) * DV]
            og = jax.nn.sigmoid(o_ref[rows, h * DV:(h + 1) * DV].astype(F32))
            z_s[rows, h * DV:(h + 1) * DV] = (og * hn).astype(BF16)

            w_row = b_last - b_row + i_row
            m_loc = jnp.max(w_row, axis=-1, keepdims=True)
            e_col = jnp.exp(b_last - b_col + i_col - m_loc)
            c_inc = lax.dot_general((k_h * e_col).astype(BF16), va_h, (((0,), (0,)), ((), ())),
                                    preferred_element_type=F32)
            m_new = jnp.maximum(b_last + m_prev, m_loc)
            c_s[h] = jnp.exp(b_last + m_prev - m_new) * c_s[h] + jnp.exp(m_loc - m_new) * c_inc
            m_s[h] = jnp.broadcast_to(m_new, (SUBLANES, LANES))

        gg = jax.nn.gelu(gv_ref[rows, :].astype(F32))
        mu = jnp.mean(gg, axis=-1, keepdims=True)
        var = jnp.mean(jnp.square(gg - mu), axis=-1, keepdims=True)
        vln = ((gg - mu) * lax.rsqrt(var + EPS) * lng_ref[...] + lnb_ref[...]).astype(BF16)
        for g in range(N_GROUPS):
            cols = slice(g * GROUP, (g + 1) * GROUP)
            zg = jnp.dot(wc_s[g], vln[:, cols], preferred_element_type=F32) + bsp_ref[:, g:g + 1]
            gu = jax.nn.gelu(u_ref[rows, cols].astype(F32))
            z_s[rows, D_MLSTM + g * GROUP:D_MLSTM + (g + 1) * GROUP] = (gu * zg).astype(BF16)
        return carry

    lax.fori_loop(0, n_chunks, chunk_body, 0)

    y_s[...] = jnp.dot(z_s[...], wout_ref[...], preferred_element_type=F32)
    _gated_residual_rows(x_ref, y_s, gpost_ref, gate_ref, 1.0, out_ref)


def _mixer_call(proj, g_col, g_row, x, gate, conv_w, conv_b, b_col, b_row, g_mh, ln_g, ln_b,
                w_sp, b_sp_t, w_out, g_post, bsz, seq, ts=512):
    t, d = x.shape
    nts = seq // ts
    nck = ts // CHUNK
    rowi = lambda k: (lambda b, s: (b * nts + s, k))
    const2 = lambda b, s: (0, 0)
    return pl.pallas_call(
        functools.partial(_mixer_kernel, ts=ts),
        grid=(bsz, nts),
        in_specs=[
            pl.BlockSpec((ts, 2 * D_QK), rowi(0)),
            pl.BlockSpec((ts, D_MLSTM), rowi(1)),
            pl.BlockSpec((ts, D_MLSTM), rowi(2)),
            pl.BlockSpec((ts, D_GMLP), rowi(3)),
            pl.BlockSpec((ts, D_GMLP), rowi(4)),
            pl.BlockSpec((ts, LANES), rowi(0)),
            pl.BlockSpec((None, nck, SUBLANES, CHUNK), lambda b, s: (b, s, 0, 0)),
            pl.BlockSpec((ts, d), rowi(0)),
            pl.BlockSpec((None, 1, d), lambda b, s: (b, 0, 0)),
            pl.BlockSpec((CONV_WIDTH, 2 * D_QK), const2),
            pl.BlockSpec((1, 2 * D_QK), const2),
            pl.BlockSpec((1, LANES), const2),
            pl.BlockSpec((SUBLANES, 1), const2),
            pl.BlockSpec((1, D_MLSTM), const2),
            pl.BlockSpec((1, D_GMLP), const2),
            pl.BlockSpec((1, D_GMLP), const2),
            pl.BlockSpec((N_GROUPS, CHUNK, CHUNK), lambda b, s: (0, 0, 0)),
            pl.BlockSpec((CHUNK, N_GROUPS), const2),
            pl.BlockSpec((d, d), const2),
            pl.BlockSpec((1, d), const2),
        ],
        out_specs=pl.BlockSpec((ts, d), rowi(0)),
        out_shape=jax.ShapeDtypeStruct((t, d), F32),
        scratch_shapes=[
            pltpu.VMEM((ts + 2 * SUBLANES, 2 * D_QK), F32),
            pltpu.VMEM((ts, D_QK), BF16),
            pltpu.VMEM((ts, D_QK), F32),
            pltpu.VMEM((ts, N_HEADS * DV_AUG), BF16),
            pltpu.VMEM((ts, d), BF16),
            pltpu.VMEM((ts, d), F32),
            pltpu.VMEM((N_GROUPS, CHUNK, CHUNK), BF16),
            pltpu.VMEM((N_HEADS, DQK, DV_AUG), F32),
            pltpu.VMEM((N_HEADS, SUBLANES, LANES), F32),
        ],
        compiler_params=pltpu.CompilerParams(
            dimension_semantics=("arbitrary", "arbitrary"), vmem_limit_bytes=VMEM_LIMIT),
        name="mixer_core",
    )(proj, proj, proj, proj, proj, g_col, g_row, x, gate, conv_w, conv_b, b_col, b_row,
      g_mh, ln_g, ln_b, w_sp, b_sp_t, w_out, g_post)


def kernel(x, c, w_ada, b_ada, g_pre, g_post, w_ff_gate, w_ff_up, w_ff_down, w_in, conv_w, conv_b,
           b_igate, b_fgate, g_mhnorm, gmlp_ln_g, gmlp_ln_b, w_spatial, b_spatial, w_out):
    bsz, seq, d = x.shape
    depth = w_ada.shape[0]
    xf = x.reshape(bsz * seq, d)
    for l in range(depth):
        mod = _ada_call(c, w_ada[l], b_ada[l]).reshape(bsz, N_SUB, N_MOD, 1, d)
        shift = lambda j: mod[:, j, 0]
        scale = lambda j: mod[:, j, 1]
        gate = lambda j: mod[:, j, 2]
        gp = lambda j: g_pre[l, j].reshape(1, d)
        gq = lambda j: g_post[l, j].reshape(1, d)

        def ffn(xin, j, idx):
            return _ffn_call(xin, shift(j), scale(j), gate(j), gp(j), gq(j),
                             w_ff_gate[l, idx].astype(BF16), w_ff_up[l, idx].astype(BF16),
                             w_ff_down[l, idx].astype(BF16), 0.5, seq)

        xf = ffn(xf, 0, 0)

        w_main = jnp.concatenate([w_in[l][:, :GATE_LO], w_in[l][:, GATE_HI:]], axis=1).astype(BF16)
        w_gate = jnp.pad(w_in[l][:, GATE_LO:GATE_HI], ((0, 0), (0, LANES - 2 * N_HEADS))).astype(BF16)
        proj, g_col = _inproj_call(xf, shift(1), scale(1), gp(1), w_main, w_gate, seq)
        g_row = (g_col[:, :SUBLANES].reshape(bsz, seq // CHUNK, CHUNK, SUBLANES)
                 .transpose(0, 1, 3, 2))
        bias = jnp.concatenate([b_igate[l], b_fgate[l]])
        b_col = jnp.pad(bias, (0, LANES - 2 * N_HEADS)).reshape(1, LANES)
        b_row = bias.reshape(SUBLANES, 1)
        xf = _mixer_call(proj, g_col, g_row, xf, gate(1), conv_w[l], conv_b[l].reshape(1, -1),
                         b_col, b_row, g_mhnorm[l].reshape(1, -1), gmlp_ln_g[l].reshape(1, -1),
                         gmlp_ln_b[l].reshape(1, -1), w_spatial[l], b_spatial[l].T,
                         w_out[l].astype(BF16), gq(1), bsz, seq)

        xf = ffn(xf, 2, 1)
    return xf.reshape(bsz, seq, d)
```

```python
import functools

import jax
import jax.numpy as jnp
from jax import lax
from jax.experimental import pallas as pl
from jax.experimental.pallas import tpu as pltpu

D_MODEL = 2048
D_MLSTM = D_MODEL // 2
N_HEADS = 4
DV = D_MLSTM // N_HEADS
DQK = DV // 2
D_QK = N_HEADS * DQK
CHUNK = 128
CONV_WIDTH = 4
D_GMLP = D_MODEL - D_MLSTM
N_GROUPS = 8
GROUP = D_GMLP // N_GROUPS
D_FF = 5632
N_SUB = 3
N_MOD = 3
EPS = 1e-6
GATE_LO = 2 * D_QK + 2 * D_MLSTM
GATE_HI = GATE_LO + 2 * N_HEADS
D_MAIN = 2 * D_QK + 2 * D_MLSTM + 2 * D_GMLP
LANES = 128
SUBLANES = 8
DV_AUG = DV + LANES

VMEM_LIMIT = 56 * 1024 * 1024

F32 = jnp.float32
BF16 = jnp.bfloat16


def _rms(y):
    return y * lax.rsqrt(jnp.mean(y * y, axis=-1, keepdims=True) + EPS)


ROW_BLOCK = 16
NORM_UNROLL = 4


def _modulated_norm_rows(x_ref, gpre_ref, scale_ref, shift_ref, h_ref):
    mult = gpre_ref[...] * (1.0 + scale_ref[...])
    shift = shift_ref[...]

    def body(r, carry):
        rows = pl.ds(pl.multiple_of(r * ROW_BLOCK, ROW_BLOCK), ROW_BLOCK)
        h_ref[rows, :] = (_rms(x_ref[rows, :]) * mult + shift).astype(h_ref.dtype)
        return carry

    lax.fori_loop(0, x_ref.shape[0] // ROW_BLOCK, body, 0, unroll=NORM_UNROLL)


def _gated_residual_rows(x_ref, y_ref, gpost_ref, gate_ref, coef, o_ref):
    mult = gpost_ref[...] * (coef * gate_ref[...])

    def body(r, carry):
        rows = pl.ds(pl.multiple_of(r * ROW_BLOCK, ROW_BLOCK), ROW_BLOCK)
        o_ref[rows, :] = x_ref[rows, :] + _rms(y_ref[rows, :]) * mult
        return carry

    lax.fori_loop(0, x_ref.shape[0] // ROW_BLOCK, body, 0, unroll=NORM_UNROLL)


def _ada_kernel(c_ref, w_ref, b_ref, o_ref):
    a = jax.nn.silu(c_ref[...]).astype(BF16)
    o_ref[...] = jnp.dot(a, w_ref[...].astype(BF16), preferred_element_type=F32) + b_ref[...]


def _ada_call(c, w, b, tn=1024):
    bsz, d = c.shape
    n = w.shape[1]
    return pl.pallas_call(
        _ada_kernel,
        grid=(n // tn,),
        in_specs=[
            pl.BlockSpec((bsz, d), lambda j: (0, 0)),
            pl.BlockSpec((d, tn), lambda j: (0, j)),
            pl.BlockSpec((1, tn), lambda j: (0, j)),
        ],
        out_specs=pl.BlockSpec((bsz, tn), lambda j: (0, j)),
        out_shape=jax.ShapeDtypeStruct((bsz, n), F32),
        compiler_params=pltpu.CompilerParams(
            dimension_semantics=("arbitrary",), vmem_limit_bytes=VMEM_LIMIT),
        name="adaln_mod",
    )(c, w, b.reshape(1, n))


SIDE_ROWS = 48


def _ffn_kernel(xn_ref, xp_ref, shiftn_ref, scalen_ref, shiftp_ref, scalep_ref, gatep_ref,
                gpre_ref, gpost_ref, wg_ref, wu_ref, wd_ref, o_ref,
                h0_ref, h1_ref, acc0_ref, acc1_ref, *, coef, n_tiles):
    i = pl.program_id(0)
    j = pl.program_id(1)
    tm = xn_ref.shape[0]
    h_refs = (h0_ref, h1_ref)
    acc_refs = (acc0_ref, acc1_ref)

    @pl.when((i == 0) & (j == 0))
    def _():
        _modulated_norm_rows(xp_ref, gpre_ref, scalep_ref, shiftp_ref, h0_ref)
        acc0_ref[...] = jnp.zeros_like(acc0_ref)
        acc1_ref[...] = jnp.zeros_like(acc1_ref)

    def tile_step(cur):
        h_cur, h_nxt = h_refs[cur], h_refs[1 - cur]
        acc_cur, acc_nxt = acc_refs[cur], acc_refs[1 - cur]

        start = jnp.minimum(j * SIDE_ROWS, tm - SIDE_ROWS)
        pre_mult = gpre_ref[...] * (1.0 + scalen_ref[...])
        pre_shift = shiftn_ref[...]
        post_mult = gpost_ref[...] * (coef * gatep_ref[...])
        for k in range(SIDE_ROWS // ROW_BLOCK):
            rows = pl.ds(pl.multiple_of(start + k * ROW_BLOCK, ROW_BLOCK), ROW_BLOCK)
            h_nxt[rows, :] = (_rms(xn_ref[rows, :]) * pre_mult + pre_shift).astype(BF16)
            o_ref[rows, :] = xp_ref[rows, :] + _rms(acc_nxt[rows, :]) * post_mult

        h = h_cur[...]
        g = jnp.dot(h, wg_ref[...], preferred_element_type=F32)
        u = jnp.dot(h, wu_ref[...], preferred_element_type=F32)
        a = (jax.nn.silu(g) * u).astype(BF16)
        acc_cur[...] = (jnp.where(j > 0, acc_cur[...], 0.0)
                        + jnp.dot(a, wd_ref[...], preferred_element_type=F32))

    for parity in range(2):
        pl.when((i < n_tiles) & (i % 2 == parity))(functools.partial(tile_step, parity))

    @pl.when((i == n_tiles) & (j == 0))
    def _():
        _gated_residual_rows(xp_ref, acc_refs[(n_tiles - 1) % 2], gpost_ref, gatep_ref, coef, o_ref)


def _ffn_call(x, shift, scale, gate, g_pre, g_post, wg, wu, wd, coef, seq, tm=512):
    t, d = x.shape
    n_ff, _, tf = wg.shape
    per_b = seq // tm
    n_tiles = t // tm
    assert (n_ff - 1) * SIDE_ROWS < tm <= n_ff * SIDE_ROWS and (tm - SIDE_ROWS) % ROW_BLOCK == 0
    nxt_tile = lambda i: jnp.minimum(i + 1, n_tiles - 1)
    prv_tile = lambda i: jnp.maximum(i - 1, 0)
    ff_tile = lambda i, j: jnp.where(i < n_tiles, j, n_ff - 1)
    const = lambda i, j: (0, 0)
    mod_n = pl.BlockSpec((None, 1, d), lambda i, j: (nxt_tile(i) // per_b, 0, 0))
    mod_p = pl.BlockSpec((None, 1, d), lambda i, j: (prv_tile(i) // per_b, 0, 0))
    return pl.pallas_call(
        functools.partial(_ffn_kernel, coef=coef, n_tiles=n_tiles),
        grid=(n_tiles + 1, n_ff),
        in_specs=[
            pl.BlockSpec((tm, d), lambda i, j: (nxt_tile(i), 0)),
            pl.BlockSpec((tm, d), lambda i, j: (prv_tile(i), 0)),
            mod_n, mod_n, mod_p, mod_p, mod_p,
            pl.BlockSpec((1, d), const),
            pl.BlockSpec((1, d), const),
            pl.BlockSpec((None, d, tf), lambda i, j: (ff_tile(i, j), 0, 0)),
            pl.BlockSpec((None, d, tf), lambda i, j: (ff_tile(i, j), 0, 0)),
            pl.BlockSpec((tf, d), lambda i, j: (ff_tile(i, j), 0)),
        ],
        out_specs=pl.BlockSpec((tm, d), lambda i, j: (prv_tile(i), 0)),
        out_shape=jax.ShapeDtypeStruct((t, d), F32),
        scratch_shapes=[pltpu.VMEM((tm, d), BF16), pltpu.VMEM((tm, d), BF16),
                        pltpu.VMEM((tm, d), F32), pltpu.VMEM((tm, d), F32)],
        compiler_params=pltpu.CompilerParams(
            dimension_semantics=("arbitrary", "arbitrary"), vmem_limit_bytes=VMEM_LIMIT),
        name="ffn_sublayer",
    )(x, x, shift, scale, shift, scale, gate, g_pre, g_post, wg, wu, wd)


def _inproj_kernel(x_ref, shift_ref, scale_ref, gpre_ref, w_ref, wgate_ref,
                   o_ref, ogate_ref, h_ref):
    j = pl.program_id(1)

    @pl.when(j == 0)
    def _():
        _modulated_norm_rows(x_ref, gpre_ref, scale_ref, shift_ref, h_ref)
        ogate_ref[...] = jnp.dot(h_ref[...], wgate_ref[...], preferred_element_type=F32)

    o_ref[...] = jnp.dot(h_ref[...], w_ref[...], preferred_element_type=F32).astype(BF16)


def _inproj_call(x, shift, scale, g_pre, w_main, w_gate, seq, tm=1024):
    t, d = x.shape
    n_col, _, tn = w_main.shape
    n = n_col * tn
    per_b = seq // tm
    modm = lambda i, j: (i // per_b, 0, 0)
    return pl.pallas_call(
        _inproj_kernel,
        grid=(t // tm, n // tn),
        in_specs=[
            pl.BlockSpec((tm, d), lambda i, j: (i, 0)),
            pl.BlockSpec((None, 1, d), modm),
            pl.BlockSpec((None, 1, d), modm),
            pl.BlockSpec((1, d), lambda i, j: (0, 0)),
            pl.BlockSpec((None, d, tn), lambda i, j: (j, 0, 0)),
            pl.BlockSpec((d, LANES), lambda i, j: (0, 0)),
        ],
        out_specs=[
            pl.BlockSpec((tm, tn), lambda i, j: (i, j)),
            pl.BlockSpec((tm, LANES), lambda i, j: (i, 0)),
        ],
        out_shape=[jax.ShapeDtypeStruct((t, n), BF16), jax.ShapeDtypeStruct((t, LANES), F32)],
        scratch_shapes=[pltpu.VMEM((tm, d), BF16)],
        compiler_params=pltpu.CompilerParams(
            dimension_semantics=("parallel", "arbitrary"), vmem_limit_bytes=VMEM_LIMIT),
        name="mixer_inproj",
    )(x, shift, scale, g_pre, w_main, w_gate)


def _mixer_kernel(qk_ref, v_ref, o_ref, u_ref, gv_ref, gcol_ref, grow_ref, x_ref, gate_ref,
                  convw_ref, convb_ref, bcol_ref, brow_ref, gmh_ref, lng_ref, lnb_ref,
                  wsp_ref, bsp_ref, wout_ref, gpost_ref,
                  out_ref,
                  qkbuf, q_s, k_s, vaug_s, z_s, y_s, wc_s, c_s, m_s, *, ts):
    s_idx = pl.program_id(1)
    n_chunks = ts // CHUNK
    tail = SUBLANES

    @pl.when(s_idx == 0)
    def _():
        qkbuf[0:tail, :] = jnp.zeros((tail, 2 * D_QK), F32)
        c_s[...] = jnp.zeros_like(c_s)
        m_s[...] = jnp.zeros_like(m_s)
        row = lax.broadcasted_iota(jnp.int32, (CHUNK, CHUNK), 0)
        col = lax.broadcasted_iota(jnp.int32, (CHUNK, CHUNK), 1)
        for g in range(N_GROUPS):
            wc_s[g] = jnp.where(col <= row, wsp_ref[g], 0.0).astype(BF16)

    qkbuf[tail:tail + ts, :] = qk_ref[...].astype(F32)
    for c in range(n_chunks):
        base = tail + c * CHUNK - (CONV_WIDTH - 1)
        acc = convb_ref[...] + convw_ref[0:1, :] * qkbuf[base:base + CHUNK, :]
        for j in range(1, CONV_WIDTH):
            acc = acc + convw_ref[j:j + 1, :] * qkbuf[base + j:base + j + CHUNK, :]
        act = jax.nn.silu(acc)
        q_s[c * CHUNK:(c + 1) * CHUNK, :] = act[:, :D_QK].astype(BF16)
        k_s[c * CHUNK:(c + 1) * CHUNK, :] = act[:, D_QK:] * (DQK ** -0.5)
    qkbuf[0:tail, :] = qkbuf[ts:ts + tail, :]

    for h in range(N_HEADS):
        vaug_s[:, h * DV_AUG:h * DV_AUG + DV] = v_ref[:, h * DV:(h + 1) * DV]
        vaug_s[:, h * DV_AUG + DV:(h + 1) * DV_AUG] = jnp.ones((ts, LANES), BF16)

    row = lax.broadcasted_iota(jnp.int32, (CHUNK, CHUNK), 0)
    col = lax.broadcasted_iota(jnp.int32, (CHUNK, CHUNK), 1)
    causal = col <= row
    tril = jnp.where(causal, 1.0, 0.0).astype(F32)
    triu = jnp.where(row <= col, 1.0, 0.0).astype(F32)

    def chunk_body(c, carry):
        r0 = pl.multiple_of(c * CHUNK, CHUNK)
        rows = pl.ds(r0, CHUNK)

        g_col = gcol_ref[rows, :] + bcol_ref[...]
        g_row = grow_ref[c] + brow_ref[...]
        bcum_col = jnp.dot(tril, jax.nn.log_sigmoid(g_col), precision=lax.Precision.HIGHEST,
                           preferred_element_type=F32)
        bcum_row = jnp.dot(jax.nn.log_sigmoid(g_row), triu, precision=lax.Precision.HIGHEST,
                           preferred_element_type=F32)

        for h in range(N_HEADS):
            i_col = g_col[:, h:h + 1]
            b_col = bcum_col[:, N_HEADS + h:N_HEADS + h + 1]
            i_row = g_row[h:h + 1, :]
            b_row = bcum_row[N_HEADS + h:N_HEADS + h + 1, :]
            b_last = b_row[:, CHUNK - 1:CHUNK]
            m_prev = m_s[h][0:1, 0:1]

            q_h = q_s[rows, h * DQK:(h + 1) * DQK]
            k_h = k_s[rows, h * DQK:(h + 1) * DQK]
            va_h = vaug_s[rows, h * DV_AUG:(h + 1) * DV_AUG]

            dmat = jnp.where(causal, b_col - b_row + i_row, -jnp.inf)
            m_intra = jnp.max(dmat, axis=-1, keepdims=True)
            m_inter = b_col + m_prev
            m_t = jnp.maximum(m_inter, m_intra)
            inter_scale = jnp.exp(m_inter - m_t)
            p = jnp.exp(dmat - m_t)
            s_qk = lax.dot_general(q_h, k_h.astype(BF16), (((1,), (1,)), ((), ())),
                                   preferred_element_type=F32) * p
            nd = (inter_scale * jnp.dot(q_h, c_s[h].astype(BF16), preferred_element_type=F32)
                  + jnp.dot(s_qk.astype(BF16), va_h, preferred_element_type=F32))
            num = nd[:, :DV]
            den = nd[:, DV:DV + 1]
            hc = num / jnp.maximum(jnp.abs(den), jnp.exp(-m_t))
            hn = _rms(hc) * gmh_ref[:, h * DV:(h + 1) * DV]
            og = jax.nn.sigmoid(o_ref[rows, h * DV:(h + 1) * DV].astype(F32))
            z_s[rows, h * DV:(h + 1) * DV] = (og * hn).astype(BF16)

            w_row = b_last - b_row + i_row
            m_loc = jnp.max(w_row, axis=-1, keepdims=True)
            e_col = jnp.exp(b_last - b_col + i_col - m_loc)
            c_inc = lax.dot_general((k_h * e_col).astype(BF16), va_h, (((0,), (0,)), ((), ())),
                                    preferred_element_type=F32)
            m_new = jnp.maximum(b_last + m_prev, m_loc)
            c_s[h] = jnp.exp(b_last + m_prev - m_new) * c_s[h] + jnp.exp(m_loc - m_new) * c_inc
            m_s[h] = jnp.broadcast_to(m_new, (SUBLANES, LANES))

        gg = jax.nn.gelu(gv_ref[rows, :].astype(F32))
        mu = jnp.mean(gg, axis=-1, keepdims=True)
        var = jnp.mean(jnp.square(gg - mu), axis=-1, keepdims=True)
        vln = ((gg - mu) * lax.rsqrt(var + EPS) * lng_ref[...] + lnb_ref[...]).astype(BF16)
        for g in range(N_GROUPS):
            cols = slice(g * GROUP, (g + 1) * GROUP)
            zg = jnp.dot(wc_s[g], vln[:, cols], preferred_element_type=F32) + bsp_ref[:, g:g + 1]
            gu = jax.nn.gelu(u_ref[rows, cols].astype(F32))
            z_s[rows, D_MLSTM + g * GROUP:D_MLSTM + (g + 1) * GROUP] = (gu * zg).astype(BF16)
        return carry

    lax.fori_loop(0, n_chunks, chunk_body, 0)

    y_s[...] = jnp.dot(z_s[...], wout_ref[...], preferred_element_type=F32)
    _gated_residual_rows(x_ref, y_s, gpost_ref, gate_ref, 1.0, out_ref)


def _mixer_call(proj, g_col, g_row, x, gate, conv_w, conv_b, b_col, b_row, g_mh, ln_g, ln_b,
                w_sp, b_sp_t, w_out, g_post, bsz, seq, ts=512):
    t, d = x.shape
    nts = seq // ts
    nck = ts // CHUNK
    rowi = lambda k: (lambda b, s: (b * nts + s, k))
    const2 = lambda b, s: (0, 0)
    return pl.pallas_call(
        functools.partial(_mixer_kernel, ts=ts),
        grid=(bsz, nts),
        in_specs=[
            pl.BlockSpec((ts, 2 * D_QK), rowi(0)),
            pl.BlockSpec((ts, D_MLSTM), rowi(1)),
            pl.BlockSpec((ts, D_MLSTM), rowi(2)),
            pl.BlockSpec((ts, D_GMLP), rowi(3)),
            pl.BlockSpec((ts, D_GMLP), rowi(4)),
            pl.BlockSpec((ts, LANES), rowi(0)),
            pl.BlockSpec((None, nck, SUBLANES, CHUNK), lambda b, s: (b, s, 0, 0)),
            pl.BlockSpec((ts, d), rowi(0)),
            pl.BlockSpec((None, 1, d), lambda b, s: (b, 0, 0)),
            pl.BlockSpec((CONV_WIDTH, 2 * D_QK), const2),
            pl.BlockSpec((1, 2 * D_QK), const2),
            pl.BlockSpec((1, LANES), const2),
            pl.BlockSpec((SUBLANES, 1), const2),
            pl.BlockSpec((1, D_MLSTM), const2),
            pl.BlockSpec((1, D_GMLP), const2),
            pl.BlockSpec((1, D_GMLP), const2),
            pl.BlockSpec((N_GROUPS, CHUNK, CHUNK), lambda b, s: (0, 0, 0)),
            pl.BlockSpec((CHUNK, N_GROUPS), const2),
            pl.BlockSpec((d, d), const2),
            pl.BlockSpec((1, d), const2),
        ],
        out_specs=pl.BlockSpec((ts, d), rowi(0)),
        out_shape=jax.ShapeDtypeStruct((t, d), F32),
        scratch_shapes=[
            pltpu.VMEM((ts + 2 * SUBLANES, 2 * D_QK), F32),
            pltpu.VMEM((ts, D_QK), BF16),
            pltpu.VMEM((ts, D_QK), F32),
            pltpu.VMEM((ts, N_HEADS * DV_AUG), BF16),
            pltpu.VMEM((ts, d), BF16),
            pltpu.VMEM((ts, d), F32),
            pltpu.VMEM((N_GROUPS, CHUNK, CHUNK), BF16),
            pltpu.VMEM((N_HEADS, DQK, DV_AUG), F32),
            pltpu.VMEM((N_HEADS, SUBLANES, LANES), F32),
        ],
        compiler_params=pltpu.CompilerParams(
            dimension_semantics=("arbitrary", "arbitrary"), vmem_limit_bytes=VMEM_LIMIT),
        name="mixer_core",
    )(proj, proj, proj, proj, proj, g_col, g_row, x, gate, conv_w, conv_b, b_col, b_row,
      g_mh, ln_g, ln_b, w_sp, b_sp_t, w_out, g_post)


FF_TILE = 512
PROJ_TILE = 1024


def _col_tiles(w, tn):
    d, n = w.shape
    return w.astype(BF16).reshape(d, n // tn, tn).transpose(1, 0, 2)


def kernel(x, c, w_ada, b_ada, g_pre, g_post, w_ff_gate, w_ff_up, w_ff_down, w_in, conv_w, conv_b,
           b_igate, b_fgate, g_mhnorm, gmlp_ln_g, gmlp_ln_b, w_spatial, b_spatial, w_out):
    bsz, seq, d = x.shape
    depth = w_ada.shape[0]
    xf = x.reshape(bsz * seq, d)
    for l in range(depth):
        mod = _ada_call(c, w_ada[l], b_ada[l]).reshape(bsz, N_SUB, N_MOD, 1, d)
        shift = lambda j: mod[:, j, 0]
        scale = lambda j: mod[:, j, 1]
        gate = lambda j: mod[:, j, 2]
        gp = lambda j: g_pre[l, j].reshape(1, d)
        gq = lambda j: g_post[l, j].reshape(1, d)

        def ffn(xin, j, idx):
            return _ffn_call(xin, shift(j), scale(j), gate(j), gp(j), gq(j),
                             _col_tiles(w_ff_gate[l, idx], FF_TILE), _col_tiles(w_ff_up[l, idx], FF_TILE),
                             w_ff_down[l, idx].astype(BF16), 0.5, seq)

        xf = ffn(xf, 0, 0)

        w_main = _col_tiles(jnp.concatenate([w_in[l][:, :GATE_LO], w_in[l][:, GATE_HI:]], axis=1),
                            PROJ_TILE)
        w_gate = jnp.pad(w_in[l][:, GATE_LO:GATE_HI], ((0, 0), (0, LANES - 2 * N_HEADS))).astype(BF16)
        proj, g_col = _inproj_call(xf, shift(1), scale(1), gp(1), w_main, w_gate, seq)
        g_row = (g_col[:, :SUBLANES].reshape(bsz, seq // CHUNK, CHUNK, SUBLANES)
                 .transpose(0, 1, 3, 2))
        bias = jnp.concatenate([b_igate[l], b_fgate[l]])
        b_col = jnp.pad(bias, (0, LANES - 2 * N_HEADS)).reshape(1, LANES)
        b_row = bias.reshape(SUBLANES, 1)
        xf = _mixer_call(proj, g_col, g_row, xf, gate(1), conv_w[l], conv_b[l].reshape(1, -1),
                         b_col, b_row, g_mhnorm[l].reshape(1, -1), gmlp_ln_g[l].reshape(1, -1),
                         gmlp_ln_b[l].reshape(1, -1), w_spatial[l], b_spatial[l].T,
                         w_out[l].astype(BF16), gq(1), bsz, seq)

        xf = ffn(xf, 2, 1)
    return xf.reshape(bsz, seq, d)
```

```python
import functools

import jax
import jax.numpy as jnp
from jax import lax
from jax.experimental import pallas as pl
from jax.experimental.pallas import tpu as pltpu

D_MODEL = 2048
D_MLSTM = D_MODEL // 2
N_HEADS = 4
DV = D_MLSTM // N_HEADS
DQK = DV // 2
D_QK = N_HEADS * DQK
CHUNK = 128
CONV_WIDTH = 4
D_GMLP = D_MODEL - D_MLSTM
N_GROUPS = 8
GROUP = D_GMLP // N_GROUPS
D_FF = 5632
N_SUB = 3
N_MOD = 3
EPS = 1e-6
GATE_LO = 2 * D_QK + 2 * D_MLSTM
GATE_HI = GATE_LO + 2 * N_HEADS
D_MAIN = 2 * D_QK + 2 * D_MLSTM + 2 * D_GMLP
LANES = 128
SUBLANES = 8
DV_AUG = DV + LANES

VMEM_LIMIT = 56 * 1024 * 1024

F32 = jnp.float32
BF16 = jnp.bfloat16


def _rms(y):
    return y * lax.rsqrt(jnp.mean(y * y, axis=-1, keepdims=True) + EPS)


ROW_BLOCK = 16
NORM_UNROLL = 4


def _modulated_norm_rows(x_ref, gpre_ref, scale_ref, shift_ref, h_ref):
    mult = gpre_ref[...] * (1.0 + scale_ref[...])
    shift = shift_ref[...]

    def body(r, carry):
        rows = pl.ds(pl.multiple_of(r * ROW_BLOCK, ROW_BLOCK), ROW_BLOCK)
        h_ref[rows, :] = (_rms(x_ref[rows, :]) * mult + shift).astype(h_ref.dtype)
        return carry

    lax.fori_loop(0, x_ref.shape[0] // ROW_BLOCK, body, 0, unroll=NORM_UNROLL)


def _gated_residual_rows(x_ref, y_ref, gpost_ref, gate_ref, coef, o_ref):
    mult = gpost_ref[...] * (coef * gate_ref[...])

    def body(r, carry):
        rows = pl.ds(pl.multiple_of(r * ROW_BLOCK, ROW_BLOCK), ROW_BLOCK)
        o_ref[rows, :] = x_ref[rows, :] + _rms(y_ref[rows, :]) * mult
        return carry

    lax.fori_loop(0, x_ref.shape[0] // ROW_BLOCK, body, 0, unroll=NORM_UNROLL)


def _ada_kernel(c_ref, w_ref, b_ref, o_ref):
    a = jax.nn.silu(c_ref[...]).astype(BF16)
    o_ref[...] = jnp.dot(a, w_ref[...].astype(BF16), preferred_element_type=F32) + b_ref[...]


def _ada_call(c, w, b, tn=1024):
    bsz, d = c.shape
    n = w.shape[1]
    return pl.pallas_call(
        _ada_kernel,
        grid=(n // tn,),
        in_specs=[
            pl.BlockSpec((bsz, d), lambda j: (0, 0)),
            pl.BlockSpec((d, tn), lambda j: (0, j)),
            pl.BlockSpec((1, tn), lambda j: (0, j)),
        ],
        out_specs=pl.BlockSpec((bsz, tn), lambda j: (0, j)),
        out_shape=jax.ShapeDtypeStruct((bsz, n), F32),
        compiler_params=pltpu.CompilerParams(
            dimension_semantics=("arbitrary",), vmem_limit_bytes=VMEM_LIMIT),
        name="adaln_mod",
    )(c, w, b.reshape(1, n))


def _ffn_kernel(x_ref, shift_ref, scale_ref, gate_ref, gpre_ref, gpost_ref,
                wg_ref, wu_ref, wd_ref, o_ref, h_ref, acc_ref, *, coef):
    j = pl.program_id(1)

    @pl.when(j == 0)
    def _():
        _modulated_norm_rows(x_ref, gpre_ref, scale_ref, shift_ref, h_ref)

    h = h_ref[...]
    g = jnp.dot(h, wg_ref[...], preferred_element_type=F32)
    u = jnp.dot(h, wu_ref[...], preferred_element_type=F32)
    a = (jax.nn.silu(g) * u).astype(BF16)
    acc_ref[...] = (jnp.where(j > 0, acc_ref[...], 0.0)
                    + jnp.dot(a, wd_ref[...], preferred_element_type=F32))

    @pl.when(j == pl.num_programs(1) - 1)
    def _():
        _gated_residual_rows(x_ref, acc_ref, gpost_ref, gate_ref, coef, o_ref)


def _ffn_call(x, shift, scale, gate, g_pre, g_post, wg, wu, wd, coef, seq, tm=1024, tf=256):
    t, d = x.shape
    ff = wg.shape[1]
    per_b = seq // tm
    row = lambda i, j: (i, 0)
    modm = lambda i, j: (i // per_b, 0, 0)
    const = lambda i, j: (0, 0)
    return pl.pallas_call(
        functools.partial(_ffn_kernel, coef=coef),
        grid=(t // tm, ff // tf),
        in_specs=[
            pl.BlockSpec((tm, d), row),
            pl.BlockSpec((None, 1, d), modm),
            pl.BlockSpec((None, 1, d), modm),
            pl.BlockSpec((None, 1, d), modm),
            pl.BlockSpec((1, d), const),
            pl.BlockSpec((1, d), const),
            pl.BlockSpec((d, tf), lambda i, j: (0, j)),
            pl.BlockSpec((d, tf), lambda i, j: (0, j)),
            pl.BlockSpec((tf, d), lambda i, j: (j, 0)),
        ],
        out_specs=pl.BlockSpec((tm, d), row),
        out_shape=jax.ShapeDtypeStruct((t, d), F32),
        scratch_shapes=[pltpu.VMEM((tm, d), BF16), pltpu.VMEM((tm, d), F32)],
        compiler_params=pltpu.CompilerParams(
            dimension_semantics=("parallel", "arbitrary"), vmem_limit_bytes=VMEM_LIMIT),
        name="ffn_sublayer",
    )(x, shift, scale, gate, g_pre, g_post, wg, wu, wd)


def _inproj_kernel(x_ref, shift_ref, scale_ref, gpre_ref, w_ref, wgate_ref,
                   o_ref, ogate_ref, h_ref):
    j = pl.program_id(1)

    @pl.when(j == 0)
    def _():
        _modulated_norm_rows(x_ref, gpre_ref, scale_ref, shift_ref, h_ref)
        ogate_ref[...] = jnp.dot(h_ref[...], wgate_ref[...], preferred_element_type=F32)

    o_ref[...] = jnp.dot(h_ref[...], w_ref[...], preferred_element_type=F32).astype(BF16)


def _inproj_call(x, shift, scale, g_pre, w_main, w_gate, seq, tm=1024, tn=1024):
    t, d = x.shape
    n = w_main.shape[1]
    per_b = seq // tm
    modm = lambda i, j: (i // per_b, 0, 0)
    return pl.pallas_call(
        _inproj_kernel,
        grid=(t // tm, n // tn),
        in_specs=[
            pl.BlockSpec((tm, d), lambda i, j: (i, 0)),
            pl.BlockSpec((None, 1, d), modm),
            pl.BlockSpec((None, 1, d), modm),
            pl.BlockSpec((1, d), lambda i, j: (0, 0)),
            pl.BlockSpec((d, tn), lambda i, j: (0, j)),
            pl.BlockSpec((d, LANES), lambda i, j: (0, 0)),
        ],
        out_specs=[
            pl.BlockSpec((tm, tn), lambda i, j: (i, j)),
            pl.BlockSpec((tm, LANES), lambda i, j: (i, 0)),
        ],
        out_shape=[jax.ShapeDtypeStruct((t, n), BF16), jax.ShapeDtypeStruct((t, LANES), F32)],
        scratch_shapes=[pltpu.VMEM((tm, d), BF16)],
        compiler_params=pltpu.CompilerParams(
            dimension_semantics=("parallel", "arbitrary"), vmem_limit_bytes=VMEM_LIMIT),
        name="mixer_inproj",
    )(x, shift, scale, g_pre, w_main, w_gate)


def _mixer_kernel(qk_ref, v_ref, o_ref, u_ref, gv_ref, gcol_ref, grow_ref, x_ref, gate_ref,
                  convw_ref, convb_ref, bcol_ref, brow_ref, gmh_ref, lng_ref, lnb_ref,
                  wsp_ref, bsp_ref, wout_ref, gpost_ref,
                  out_ref,
                  qkbuf, q_s, k_s, vaug_s, z_s, y_s, wc_s, c_s, m_s, *, ts):
    s_idx = pl.program_id(1)
    n_chunks = ts // CHUNK
    tail = SUBLANES

    @pl.when(s_idx == 0)
    def _():
        qkbuf[0:tail, :] = jnp.zeros((tail, 2 * D_QK), F32)
        c_s[...] = jnp.zeros_like(c_s)
        m_s[...] = jnp.zeros_like(m_s)
        row = lax.broadcasted_iota(jnp.int32, (CHUNK, CHUNK), 0)
        col = lax.broadcasted_iota(jnp.int32, (CHUNK, CHUNK), 1)
        for g in range(N_GROUPS):
            wc_s[g] = jnp.where(col <= row, wsp_ref[g], 0.0).astype(BF16)

    qkbuf[tail:tail + ts, :] = qk_ref[...].astype(F32)
    for c in range(n_chunks):
        base = tail + c * CHUNK - (CONV_WIDTH - 1)
        acc = convb_ref[...] + convw_ref[0:1, :] * qkbuf[base:base + CHUNK, :]
        for j in range(1, CONV_WIDTH):
            acc = acc + convw_ref[j:j + 1, :] * qkbuf[base + j:base + j + CHUNK, :]
        act = jax.nn.silu(acc)
        q_s[c * CHUNK:(c + 1) * CHUNK, :] = act[:, :D_QK].astype(BF16)
        k_s[c * CHUNK:(c + 1) * CHUNK, :] = act[:, D_QK:] * (DQK ** -0.5)
    qkbuf[0:tail, :] = qkbuf[ts:ts + tail, :]

    for h in range(N_HEADS):
        vaug_s[:, h * DV_AUG:h * DV_AUG + DV] = v_ref[:, h * DV:(h + 1) * DV]
        vaug_s[:, h * DV_AUG + DV:(h + 1) * DV_AUG] = jnp.ones((ts, LANES), BF16)

    row = lax.broadcasted_iota(jnp.int32, (CHUNK, CHUNK), 0)
    col = lax.broadcasted_iota(jnp.int32, (CHUNK, CHUNK), 1)
    causal = col <= row
    tril = jnp.where(causal, 1.0, 0.0).astype(F32)
    triu = jnp.where(row <= col, 1.0, 0.0).astype(F32)

    def chunk_body(c, carry):
        r0 = pl.multiple_of(c * CHUNK, CHUNK)
        rows = pl.ds(r0, CHUNK)

        g_col = gcol_ref[rows, :] + bcol_ref[...]
        g_row = grow_ref[c] + brow_ref[...]
        bcum_col = jnp.dot(tril, jax.nn.log_sigmoid(g_col), precision=lax.Precision.HIGHEST,
                           preferred_element_type=F32)
        bcum_row = jnp.dot(jax.nn.log_sigmoid(g_row), triu, precision=lax.Precision.HIGHEST,
                           preferred_element_type=F32)

        for h in range(N_HEADS):
            i_col = g_col[:, h:h + 1]
            b_col = bcum_col[:, N_HEADS + h:N_HEADS + h + 1]
            i_row = g_row[h:h + 1, :]
            b_row = bcum_row[N_HEADS + h:N_HEADS + h + 1, :]
            b_last = b_row[:, CHUNK - 1:CHUNK]
            m_prev = m_s[h][0:1, 0:1]

            q_h = q_s[rows, h * DQK:(h + 1) * DQK]
            k_h = k_s[rows, h * DQK:(h + 1) * DQK]
            va_h = vaug_s[rows, h * DV_AUG:(h + 1) * DV_AUG]

            dmat = jnp.where(causal, b_col - b_row + i_row, -jnp.inf)
            m_intra = jnp.max(dmat, axis=-1, keepdims=True)
            m_inter = b_col + m_prev
            m_t = jnp.maximum(m_inter, m_intra)
            inter_scale = jnp.exp(m_inter - m_t)
            p = jnp.exp(dmat - m_t)
            s_qk = lax.dot_general(q_h, k_h.astype(BF16), (((1,), (1,)), ((), ())),
                                   preferred_element_type=F32) * p
            nd = (inter_scale * jnp.dot(q_h, c_s[h].astype(BF16), preferred_element_type=F32)
                  + jnp.dot(s_qk.astype(BF16), va_h, preferred_element_type=F32))
            num = nd[:, :DV]
            den = nd[:, DV:DV + 1]
            hc = num / jnp.maximum(jnp.abs(den), jnp.exp(-m_t))
            hn = _rms(hc) * gmh_ref[:, h * DV:(h + 1) * DV]
            og = jax.nn.sigmoid(o_ref[rows, h * DV:(h + 1) * DV].astype(F32))
            z_s[rows, h * DV:(h + 1) * DV] = (og * hn).astype(BF16)

            w_row = b_last - b_row + i_row
            m_loc = jnp.max(w_row, axis=-1, keepdims=True)
            e_col = jnp.exp(b_last - b_col + i_col - m_loc)
            c_inc = lax.dot_general((k_h * e_col).astype(BF16), va_h, (((0,), (0,)), ((), ())),
                                    preferred_element_type=F32)
            m_new = jnp.maximum(b_last + m_prev, m_loc)
            c_s[h] = jnp.exp(b_last + m_prev - m_new) * c_s[h] + jnp.exp(m_loc - m_new) * c_inc
            m_s[h] = jnp.broadcast_to(m_new, (SUBLANES, LANES))

        gg = jax.nn.gelu(gv_ref[rows, :].astype(F32))
        mu = jnp.mean(gg, axis=-1, keepdims=True)
        var = jnp.mean(jnp.square(gg - mu), axis=-1, keepdims=True)
        vln = ((gg - mu) * lax.rsqrt(var + EPS) * lng_ref[...] + lnb_ref[...]).astype(BF16)
        for g in range(N_GROUPS):
            cols = slice(g * GROUP, (g + 1) * GROUP)
            zg = jnp.dot(wc_s[g], vln[:, cols], preferred_element_type=F32) + bsp_ref[:, g:g + 1]
            gu = jax.nn.gelu(u_ref[rows, cols].astype(F32))
            z_s[rows, D_MLSTM + g * GROUP:D_MLSTM + (g + 1) * GROUP] = (gu * zg).astype(BF16)
        return carry

    lax.fori_loop(0, n_chunks, chunk_body, 0)

    y_s[...] = jnp.dot(z_s[...], wout_ref[...], preferred_element_type=F32)
    _gated_residual_rows(x_ref, y_s, gpost_ref, gate_ref, 1.0, out_ref)


def _mixer_call(proj, g_col, g_row, x, gate, conv_w, conv_b, b_col, b_row, g_mh, ln_g, ln_b,
                w_sp, b_sp_t, w_out, g_post, bsz, seq, ts=512):
    t, d = x.shape
    nts = seq // ts
    nck = ts // CHUNK
    rowi = lambda k: (lambda b, s: (b * nts + s, k))
    const2 = lambda b, s: (0, 0)
    return pl.pallas_call(
        functools.partial(_mixer_kernel, ts=ts),
        grid=(bsz, nts),
        in_specs=[
            pl.BlockSpec((ts, 2 * D_QK), rowi(0)),
            pl.BlockSpec((ts, D_MLSTM), rowi(1)),
            pl.BlockSpec((ts, D_MLSTM), rowi(2)),
            pl.BlockSpec((ts, D_GMLP), rowi(3)),
            pl.BlockSpec((ts, D_GMLP), rowi(4)),
            pl.BlockSpec((ts, LANES), rowi(0)),
            pl.BlockSpec((None, nck, SUBLANES, CHUNK), lambda b, s: (b, s, 0, 0)),
            pl.BlockSpec((ts, d), rowi(0)),
            pl.BlockSpec((None, 1, d), lambda b, s: (b, 0, 0)),
            pl.BlockSpec((CONV_WIDTH, 2 * D_QK), const2),
            pl.BlockSpec((1, 2 * D_QK), const2),
            pl.BlockSpec((1, LANES), const2),
            pl.BlockSpec((SUBLANES, 1), const2),
            pl.BlockSpec((1, D_MLSTM), const2),
            pl.BlockSpec((1, D_GMLP), const2),
            pl.BlockSpec((1, D_GMLP), const2),
            pl.BlockSpec((N_GROUPS, CHUNK, CHUNK), lambda b, s: (0, 0, 0)),
            pl.BlockSpec((CHUNK, N_GROUPS), const2),
            pl.BlockSpec((d, d), const2),
            pl.BlockSpec((1, d), const2),
        ],
        out_specs=pl.BlockSpec((ts, d), rowi(0)),
        out_shape=jax.ShapeDtypeStruct((t, d), F32),
        scratch_shapes=[
            pltpu.VMEM((ts + 2 * SUBLANES, 2 * D_QK), F32),
            pltpu.VMEM((ts, D_QK), BF16),
            pltpu.VMEM((ts, D_QK), F32),
            pltpu.VMEM((ts, N_HEADS * DV_AUG), BF16),
            pltpu.VMEM((ts, d), BF16),
            pltpu.VMEM((ts, d), F32),
            pltpu.VMEM((N_GROUPS, CHUNK, CHUNK), BF16),
            pltpu.VMEM((N_HEADS, DQK, DV_AUG), F32),
            pltpu.VMEM((N_HEADS, SUBLANES, LANES), F32),
        ],
        compiler_params=pltpu.CompilerParams(
            dimension_semantics=("arbitrary", "arbitrary"), vmem_limit_bytes=VMEM_LIMIT),
        name="mixer_core",
    )(proj, proj, proj, proj, proj, g_col, g_row, x, gate, conv_w, conv_b, b_col, b_row,
      g_mh, ln_g, ln_b, w_sp, b_sp_t, w_out, g_post)


def kernel(x, c, w_ada, b_ada, g_pre, g_post, w_ff_gate, w_ff_up, w_ff_down, w_in, conv_w, conv_b,
           b_igate, b_fgate, g_mhnorm, gmlp_ln_g, gmlp_ln_b, w_spatial, b_spatial, w_out):
    bsz, seq, d = x.shape
    depth = w_ada.shape[0]
    xf = x.reshape(bsz * seq, d)
    for l in range(depth):
        mod = _ada_call(c, w_ada[l], b_ada[l]).reshape(bsz, N_SUB, N_MOD, 1, d)
        shift = lambda j: mod[:, j, 0]
        scale = lambda j: mod[:, j, 1]
        gate = lambda j: mod[:, j, 2]
        gp = lambda j: g_pre[l, j].reshape(1, d)
        gq = lambda j: g_post[l, j].reshape(1, d)

        def ffn(xin, j, idx):
            return _ffn_call(xin, shift(j), scale(j), gate(j), gp(j), gq(j),
                             w_ff_gate[l, idx].astype(BF16), w_ff_up[l, idx].astype(BF16),
                             w_ff_down[l, idx].astype(BF16), 0.5, seq)

        xf = ffn(xf, 0, 0)

        w_main = jnp.concatenate([w_in[l][:, :GATE_LO], w_in[l][:, GATE_HI:]], axis=1).astype(BF16)
        w_gate = jnp.pad(w_in[l][:, GATE_LO:GATE_HI], ((0, 0), (0, LANES - 2 * N_HEADS))).astype(BF16)
        proj, g_col = _inproj_call(xf, shift(1), scale(1), gp(1), w_main, w_gate, seq)
        g_row = (g_col[:, :SUBLANES].reshape(bsz, seq // CHUNK, CHUNK, SUBLANES)
                 .transpose(0, 1, 3, 2))
        bias = jnp.concatenate([b_igate[l], b_fgate[l]])
        b_col = jnp.pad(bias, (0, LANES - 2 * N_HEADS)).reshape(1, LANES)
        b_row = bias.reshape(SUBLANES, 1)
        xf = _mixer_call(proj, g_col, g_row, xf, gate(1), conv_w[l], conv_b[l].reshape(1, -1),
                         b_col, b_row, g_mhnorm[l].reshape(1, -1), gmlp_ln_g[l].reshape(1, -1),
                         gmlp_ln_b[l].reshape(1, -1), w_spatial[l], b_spatial[l].T,
                         w_out[l].astype(BF16), gq(1), bsz, seq)

        xf = ffn(xf, 2, 1)
    return xf.reshape(bsz, seq, d)
```

```python
import functools

import jax
import jax.numpy as jnp
from jax import lax
from jax.experimental import pallas as pl
from jax.experimental.pallas import tpu as pltpu

D_MODEL = 2048
D_MLSTM = D_MODEL // 2
N_HEADS = 4
DV = D_MLSTM // N_HEADS
DQK = DV // 2
D_QK = N_HEADS * DQK
CHUNK = 128
CONV_WIDTH = 4
D_GMLP = D_MODEL - D_MLSTM
N_GROUPS = 8
GROUP = D_GMLP // N_GROUPS
D_FF = 5632
N_SUB = 3
N_MOD = 3
EPS = 1e-6
GATE_LO = 2 * D_QK + 2 * D_MLSTM
GATE_HI = GATE_LO + 2 * N_HEADS
D_MAIN = 2 * D_QK + 2 * D_MLSTM + 2 * D_GMLP
LANES = 128
SUBLANES = 8
DV_AUG = DV + LANES

VMEM_LIMIT = 56 * 1024 * 1024

F32 = jnp.float32
BF16 = jnp.bfloat16


def _rms(y):
    return y * lax.rsqrt(jnp.mean(y * y, axis=-1, keepdims=True) + EPS)


ROW_BLOCK = 16
NORM_UNROLL = 4


def _modulated_norm_rows(x_ref, gpre_ref, scale_ref, shift_ref, h_ref):
    mult = gpre_ref[...] * (1.0 + scale_ref[...])
    shift = shift_ref[...]

    def body(r, carry):
        rows = pl.ds(pl.multiple_of(r * ROW_BLOCK, ROW_BLOCK), ROW_BLOCK)
        h_ref[rows, :] = (_rms(x_ref[rows, :]) * mult + shift).astype(h_ref.dtype)
        return carry

    lax.fori_loop(0, x_ref.shape[0] // ROW_BLOCK, body, 0, unroll=NORM_UNROLL)


def _gated_residual_rows(x_ref, y_ref, gpost_ref, gate_ref, coef, o_ref):
    mult = gpost_ref[...] * (coef * gate_ref[...])

    def body(r, carry):
        rows = pl.ds(pl.multiple_of(r * ROW_BLOCK, ROW_BLOCK), ROW_BLOCK)
        o_ref[rows, :] = x_ref[rows, :] + _rms(y_ref[rows, :]) * mult
        return carry

    lax.fori_loop(0, x_ref.shape[0] // ROW_BLOCK, body, 0, unroll=NORM_UNROLL)


def _ada_kernel(c_ref, w_ref, b_ref, o_ref):
    a = jax.nn.silu(c_ref[...]).astype(BF16)
    o_ref[...] = jnp.dot(a, w_ref[...].astype(BF16), preferred_element_type=F32) + b_ref[...]


def _ada_call(c, w, b, tn=1024):
    bsz, d = c.shape
    n = w.shape[1]
    return pl.pallas_call(
        _ada_kernel,
        grid=(n // tn,),
        in_specs=[
            pl.BlockSpec((bsz, d), lambda j: (0, 0)),
            pl.BlockSpec((d, tn), lambda j: (0, j)),
            pl.BlockSpec((1, tn), lambda j: (0, j)),
        ],
        out_specs=pl.BlockSpec((bsz, tn), lambda j: (0, j)),
        out_shape=jax.ShapeDtypeStruct((bsz, n), F32),
        compiler_params=pltpu.CompilerParams(
            dimension_semantics=("arbitrary",), vmem_limit_bytes=VMEM_LIMIT),
        name="adaln_mod",
    )(c, w, b.reshape(1, n))


def _ffn_kernel(x_ref, shift_ref, scale_ref, gate_ref, gpre_ref, gpost_ref,
                wg_ref, wu_ref, wd_ref, o_ref, h_ref, acc_ref, *, coef):
    j = pl.program_id(1)

    @pl.when(j == 0)
    def _():
        _modulated_norm_rows(x_ref, gpre_ref, scale_ref, shift_ref, h_ref)

    h = h_ref[...]
    g = jnp.dot(h, wg_ref[...], preferred_element_type=F32)
    u = jnp.dot(h, wu_ref[...], preferred_element_type=F32)
    a = (jax.nn.silu(g) * u).astype(BF16)
    acc_ref[...] = (jnp.where(j > 0, acc_ref[...], 0.0)
                    + jnp.dot(a, wd_ref[...], preferred_element_type=F32))

    @pl.when(j == pl.num_programs(1) - 1)
    def _():
        _gated_residual_rows(x_ref, acc_ref, gpost_ref, gate_ref, coef, o_ref)


def _ffn_call(x, shift, scale, gate, g_pre, g_post, wg, wu, wd, coef, seq, tm=512, tf=512):
    t, d = x.shape
    ff = wg.shape[1]
    per_b = seq // tm
    row = lambda i, j: (i, 0)
    modm = lambda i, j: (i // per_b, 0, 0)
    const = lambda i, j: (0, 0)
    return pl.pallas_call(
        functools.partial(_ffn_kernel, coef=coef),
        grid=(t // tm, ff // tf),
        in_specs=[
            pl.BlockSpec((tm, d), row),
            pl.BlockSpec((None, 1, d), modm),
            pl.BlockSpec((None, 1, d), modm),
            pl.BlockSpec((None, 1, d), modm),
            pl.BlockSpec((1, d), const),
            pl.BlockSpec((1, d), const),
            pl.BlockSpec((d, tf), lambda i, j: (0, j)),
            pl.BlockSpec((d, tf), lambda i, j: (0, j)),
            pl.BlockSpec((tf, d), lambda i, j: (j, 0)),
        ],
        out_specs=pl.BlockSpec((tm, d), row),
        out_shape=jax.ShapeDtypeStruct((t, d), F32),
        scratch_shapes=[pltpu.VMEM((tm, d), BF16), pltpu.VMEM((tm, d), F32)],
        compiler_params=pltpu.CompilerParams(
            dimension_semantics=("parallel", "arbitrary"), vmem_limit_bytes=VMEM_LIMIT),
        name="ffn_sublayer",
    )(x, shift, scale, gate, g_pre, g_post, wg, wu, wd)


def _inproj_kernel(x_ref, shift_ref, scale_ref, gpre_ref, w_ref, wgate_ref,
                   o_ref, ogate_ref, h_ref):
    j = pl.program_id(1)

    @pl.when(j == 0)
    def _():
        _modulated_norm_rows(x_ref, gpre_ref, scale_ref, shift_ref, h_ref)
        ogate_ref[...] = jnp.dot(h_ref[...], wgate_ref[...], preferred_element_type=F32)

    o_ref[...] = jnp.dot(h_ref[...], w_ref[...], preferred_element_type=F32).astype(BF16)


def _inproj_call(x, shift, scale, g_pre, w_main, w_gate, seq, tm=1024, tn=1024):
    t, d = x.shape
    n = w_main.shape[1]
    per_b = seq // tm
    modm = lambda i, j: (i // per_b, 0, 0)
    return pl.pallas_call(
        _inproj_kernel,
        grid=(t // tm, n // tn),
        in_specs=[
            pl.BlockSpec((tm, d), lambda i, j: (i, 0)),
            pl.BlockSpec((None, 1, d), modm),
            pl.BlockSpec((None, 1, d), modm),
            pl.BlockSpec((1, d), lambda i, j: (0, 0)),
            pl.BlockSpec((d, tn), lambda i, j: (0, j)),
            pl.BlockSpec((d, LANES), lambda i, j: (0, 0)),
        ],
        out_specs=[
            pl.BlockSpec((tm, tn), lambda i, j: (i, j)),
            pl.BlockSpec((tm, LANES), lambda i, j: (i, 0)),
        ],
        out_shape=[jax.ShapeDtypeStruct((t, n), BF16), jax.ShapeDtypeStruct((t, LANES), F32)],
        scratch_shapes=[pltpu.VMEM((tm, d), BF16)],
        compiler_params=pltpu.CompilerParams(
            dimension_semantics=("parallel", "arbitrary"), vmem_limit_bytes=VMEM_LIMIT),
        name="mixer_inproj",
    )(x, shift, scale, g_pre, w_main, w_gate)


def _mixer_kernel(qk_ref, v_ref, o_ref, u_ref, gv_ref, gcol_ref, x_ref, gate_ref,
                  convw_ref, convb_ref, bcol_ref, gmh_ref, lng_ref, lnb_ref,
                  wsp_ref, bsp_ref, wout_ref, gpost_ref,
                  out_ref,
                  qkbuf, q_s, k_s, vaug_s, z_s, y_s, wc_s, c_s, m_s, *, ts):
    s_idx = pl.program_id(1)
    n_chunks = ts // CHUNK
    tail = SUBLANES

    @pl.when(s_idx == 0)
    def _():
        qkbuf[0:tail, :] = jnp.zeros((tail, 2 * D_QK), F32)
        c_s[...] = jnp.zeros_like(c_s)
        m_s[...] = jnp.zeros_like(m_s)
        row = lax.broadcasted_iota(jnp.int32, (CHUNK, CHUNK), 0)
        col = lax.broadcasted_iota(jnp.int32, (CHUNK, CHUNK), 1)
        for g in range(N_GROUPS):
            wc_s[g] = jnp.where(col <= row, wsp_ref[g], 0.0).astype(BF16)

    qkbuf[tail:tail + ts, :] = qk_ref[...].astype(F32)
    for c in range(n_chunks):
        base = tail + c * CHUNK - (CONV_WIDTH - 1)
        acc = convb_ref[...] + convw_ref[0:1, :] * qkbuf[base:base + CHUNK, :]
        for j in range(1, CONV_WIDTH):
            acc = acc + convw_ref[j:j + 1, :] * qkbuf[base + j:base + j + CHUNK, :]
        act = jax.nn.silu(acc)
        q_s[c * CHUNK:(c + 1) * CHUNK, :] = act[:, :D_QK].astype(BF16)
        k_s[c * CHUNK:(c + 1) * CHUNK, :] = act[:, D_QK:] * (DQK ** -0.5)
    qkbuf[0:tail, :] = qkbuf[ts:ts + tail, :]

    for h in range(N_HEADS):
        vaug_s[:, h * DV_AUG:h * DV_AUG + DV] = v_ref[:, h * DV:(h + 1) * DV]
        vaug_s[:, h * DV_AUG + DV:(h + 1) * DV_AUG] = jnp.ones((ts, LANES), BF16)

    row = lax.broadcasted_iota(jnp.int32, (CHUNK, CHUNK), 0)
    col = lax.broadcasted_iota(jnp.int32, (CHUNK, CHUNK), 1)
    causal = col <= row
    tril = jnp.where(causal, 1.0, 0.0).astype(F32)

    def chunk_body(c, carry):
        r0 = pl.multiple_of(c * CHUNK, CHUNK)
        rows = pl.ds(r0, CHUNK)

        g_col = gcol_ref[rows, :] + bcol_ref[...]
        bcum_col = jnp.dot(tril, jax.nn.log_sigmoid(g_col), precision=lax.Precision.HIGHEST,
                           preferred_element_type=F32)
        g_row = jnp.transpose(g_col)
        bcum_row = jnp.transpose(bcum_col)

        for h in range(N_HEADS):
            i_col = g_col[:, h:h + 1]
            b_col = bcum_col[:, N_HEADS + h:N_HEADS + h + 1]
            i_row = g_row[h:h + 1, :]
            b_row = bcum_row[N_HEADS + h:N_HEADS + h + 1, :]
            b_last = b_row[:, CHUNK - 1:CHUNK]
            m_prev = m_s[h][0:1, 0:1]

            q_h = q_s[rows, h * DQK:(h + 1) * DQK]
            k_h = k_s[rows, h * DQK:(h + 1) * DQK]
            va_h = vaug_s[rows, h * DV_AUG:(h + 1) * DV_AUG]

            dmat = jnp.where(causal, b_col - b_row + i_row, -jnp.inf)
            m_intra = jnp.max(dmat, axis=-1, keepdims=True)
            m_inter = b_col + m_prev
            m_t = jnp.maximum(m_inter, m_intra)
            inter_scale = jnp.exp(m_inter - m_t)
            p = jnp.exp(dmat - m_t)
            s_qk = lax.dot_general(q_h, k_h.astype(BF16), (((1,), (1,)), ((), ())),
                                   preferred_element_type=F32) * p
            nd = (inter_scale * jnp.dot(q_h, c_s[h].astype(BF16), preferred_element_type=F32)
                  + jnp.dot(s_qk.astype(BF16), va_h, preferred_element_type=F32))
            num = nd[:, :DV]
            den = nd[:, DV:DV + 1]
            hc = num / jnp.maximum(jnp.abs(den), jnp.exp(-m_t))
            hn = _rms(hc) * gmh_ref[:, h * DV:(h + 1) * DV]
            og = jax.nn.sigmoid(o_ref[rows, h * DV:(h + 1) * DV].astype(F32))
            z_s[rows, h * DV:(h + 1) * DV] = (og * hn).astype(BF16)

            w_row = b_last - b_row + i_row
            m_loc = jnp.max(w_row, axis=-1, keepdims=True)
            e_col = jnp.exp(b_last - b_col + i_col - m_loc)
            c_inc = lax.dot_general((k_h * e_col).astype(BF16), va_h, (((0,), (0,)), ((), ())),
                                    preferred_element_type=F32)
            m_new = jnp.maximum(b_last + m_prev, m_loc)
            c_s[h] = jnp.exp(b_last + m_prev - m_new) * c_s[h] + jnp.exp(m_loc - m_new) * c_inc
            m_s[h] = jnp.broadcast_to(m_new, (SUBLANES, LANES))

        gg = jax.nn.gelu(gv_ref[rows, :].astype(F32))
        mu = jnp.mean(gg, axis=-1, keepdims=True)
        var = jnp.mean(jnp.square(gg - mu), axis=-1, keepdims=True)
        vln = ((gg - mu) * lax.rsqrt(var + EPS) * lng_ref[...] + lnb_ref[...]).astype(BF16)
        for g in range(N_GROUPS):
            cols = slice(g * GROUP, (g + 1) * GROUP)
            zg = jnp.dot(wc_s[g], vln[:, cols], preferred_element_type=F32) + bsp_ref[:, g:g + 1]
            gu = jax.nn.gelu(u_ref[rows, cols].astype(F32))
            z_s[rows, D_MLSTM + g * GROUP:D_MLSTM + (g + 1) * GROUP] = (gu * zg).astype(BF16)
        return carry

    lax.fori_loop(0, n_chunks, chunk_body, 0)

    y_s[...] = jnp.dot(z_s[...], wout_ref[...], preferred_element_type=F32)
    _gated_residual_rows(x_ref, y_s, gpost_ref, gate_ref, 1.0, out_ref)


def _mixer_call(proj, g_col, x, gate, conv_w, conv_b, b_col, g_mh, ln_g, ln_b,
                w_sp, b_sp_t, w_out, g_post, bsz, seq, ts=512):
    t, d = x.shape
    nts = seq // ts
    rowi = lambda k: (lambda b, s: (b * nts + s, k))
    const2 = lambda b, s: (0, 0)
    return pl.pallas_call(
        functools.partial(_mixer_kernel, ts=ts),
        grid=(bsz, nts),
        in_specs=[
            pl.BlockSpec((ts, 2 * D_QK), rowi(0)),
            pl.BlockSpec((ts, D_MLSTM), rowi(1)),
            pl.BlockSpec((ts, D_MLSTM), rowi(2)),
            pl.BlockSpec((ts, D_GMLP), rowi(3)),
            pl.BlockSpec((ts, D_GMLP), rowi(4)),
            pl.BlockSpec((ts, LANES), rowi(0)),
            pl.BlockSpec((ts, d), rowi(0)),
            pl.BlockSpec((None, 1, d), lambda b, s: (b, 0, 0)),
            pl.BlockSpec((CONV_WIDTH, 2 * D_QK), const2),
            pl.BlockSpec((1, 2 * D_QK), const2),
            pl.BlockSpec((1, LANES), const2),
            pl.BlockSpec((1, D_MLSTM), const2),
            pl.BlockSpec((1, D_GMLP), const2),
            pl.BlockSpec((1, D_GMLP), const2),
            pl.BlockSpec((N_GROUPS, CHUNK, CHUNK), lambda b, s: (0, 0, 0)),
            pl.BlockSpec((CHUNK, N_GROUPS), const2),
            pl.BlockSpec((d, d), const2),
            pl.BlockSpec((1, d), const2),
        ],
        out_specs=pl.BlockSpec((ts, d), rowi(0)),
        out_shape=jax.ShapeDtypeStruct((t, d), F32),
        scratch_shapes=[
            pltpu.VMEM((ts + 2 * SUBLANES, 2 * D_QK), F32),
            pltpu.VMEM((ts, D_QK), BF16),
            pltpu.VMEM((ts, D_QK), F32),
            pltpu.VMEM((ts, N_HEADS * DV_AUG), BF16),
            pltpu.VMEM((ts, d), BF16),
            pltpu.VMEM((ts, d), F32),
            pltpu.VMEM((N_GROUPS, CHUNK, CHUNK), BF16),
            pltpu.VMEM((N_HEADS, DQK, DV_AUG), F32),
            pltpu.VMEM((N_HEADS, SUBLANES, LANES), F32),
        ],
        compiler_params=pltpu.CompilerParams(
            dimension_semantics=("arbitrary", "arbitrary"), vmem_limit_bytes=VMEM_LIMIT),
        name="mixer_core",
    )(proj, proj, proj, proj, proj, g_col, x, gate, conv_w, conv_b, b_col,
      g_mh, ln_g, ln_b, w_sp, b_sp_t, w_out, g_post)


def kernel(x, c, w_ada, b_ada, g_pre, g_post, w_ff_gate, w_ff_up, w_ff_down, w_in, conv_w, conv_b,
           b_igate, b_fgate, g_mhnorm, gmlp_ln_g, gmlp_ln_b, w_spatial, b_spatial, w_out):
    bsz, seq, d = x.shape
    depth = w_ada.shape[0]
    xf = x.reshape(bsz * seq, d)
    for l in range(depth):
        mod = _ada_call(c, w_ada[l], b_ada[l]).reshape(bsz, N_SUB, N_MOD, 1, d)
        shift = lambda j: mod[:, j, 0]
        scale = lambda j: mod[:, j, 1]
        gate = lambda j: mod[:, j, 2]
        gp = lambda j: g_pre[l, j].reshape(1, d)
        gq = lambda j: g_post[l, j].reshape(1, d)

        def ffn(xin, j, idx):
            return _ffn_call(xin, shift(j), scale(j), gate(j), gp(j), gq(j),
                             w_ff_gate[l, idx].astype(BF16), w_ff_up[l, idx].astype(BF16),
                             w_ff_down[l, idx].astype(BF16), 0.5, seq)

        xf = ffn(xf, 0, 0)

        w_main = jnp.concatenate([w_in[l][:, :GATE_LO], w_in[l][:, GATE_HI:]], axis=1).astype(BF16)
        w_gate = jnp.pad(w_in[l][:, GATE_LO:GATE_HI], ((0, 0), (0, LANES - 2 * N_HEADS))).astype(BF16)
        proj, g_col = _inproj_call(xf, shift(1), scale(1), gp(1), w_main, w_gate, seq)
        bias = jnp.concatenate([b_igate[l], b_fgate[l]])
        b_col = jnp.pad(bias, (0, LANES - 2 * N_HEADS)).reshape(1, LANES)
        xf = _mixer_call(proj, g_col, xf, gate(1), conv_w[l], conv_b[l].reshape(1, -1),
                         b_col, g_mhnorm[l].reshape(1, -1), gmlp_ln_g[l].reshape(1, -1),
                         gmlp_ln_b[l].reshape(1, -1), w_spatial[l], b_spatial[l].T,
                         w_out[l].astype(BF16), gq(1), bsz, seq)

        xf = ffn(xf, 2, 1)
    return xf.reshape(bsz, seq, d)
```

```python
import functools

import jax
import jax.numpy as jnp
from jax import lax
from jax.experimental import pallas as pl
from jax.experimental.pallas import tpu as pltpu

D_MODEL = 2048
D_MLSTM = D_MODEL // 2
N_HEADS = 4
DV = D_MLSTM // N_HEADS
DQK = DV // 2
D_QK = N_HEADS * DQK
CHUNK = 128
CONV_WIDTH = 4
D_GMLP = D_MODEL - D_MLSTM
N_GROUPS = 8
GROUP = D_GMLP // N_GROUPS
D_FF = 5632
N_SUB = 3
N_MOD = 3
EPS = 1e-6
GATE_LO = 2 * D_QK + 2 * D_MLSTM
GATE_HI = GATE_LO + 2 * N_HEADS
D_MAIN = 2 * D_QK + 2 * D_MLSTM + 2 * D_GMLP
LANES = 128
SUBLANES = 8
DV_AUG = DV + LANES

VMEM_LIMIT = 56 * 1024 * 1024

F32 = jnp.float32
BF16 = jnp.bfloat16


def _rms(y):
    return y * lax.rsqrt(jnp.mean(y * y, axis=-1, keepdims=True) + EPS)


ROW_BLOCK = 16
NORM_UNROLL = 8


def _modulated_norm_rows(x_ref, gpre_ref, scale_ref, shift_ref, h_ref):
    mult = gpre_ref[...] * (1.0 + scale_ref[...])
    shift = shift_ref[...]

    def body(r, carry):
        rows = pl.ds(pl.multiple_of(r * ROW_BLOCK, ROW_BLOCK), ROW_BLOCK)
        h_ref[rows, :] = (_rms(x_ref[rows, :]) * mult + shift).astype(h_ref.dtype)
        return carry

    lax.fori_loop(0, x_ref.shape[0] // ROW_BLOCK, body, 0, unroll=NORM_UNROLL)


def _gated_residual_rows(x_ref, y_ref, gpost_ref, gate_ref, coef, o_ref):
    mult = gpost_ref[...] * (coef * gate_ref[...])

    def body(r, carry):
        rows = pl.ds(pl.multiple_of(r * ROW_BLOCK, ROW_BLOCK), ROW_BLOCK)
        o_ref[rows, :] = x_ref[rows, :] + _rms(y_ref[rows, :]) * mult
        return carry

    lax.fori_loop(0, x_ref.shape[0] // ROW_BLOCK, body, 0, unroll=NORM_UNROLL)


def _ada_kernel(c_ref, w_ref, b_ref, o_ref):
    a = jax.nn.silu(c_ref[...]).astype(BF16)
    o_ref[...] = jnp.dot(a, w_ref[...].astype(BF16), preferred_element_type=F32) + b_ref[...]


def _ada_call(c, w, b, tn=1024):
    bsz, d = c.shape
    n = w.shape[1]
    return pl.pallas_call(
        _ada_kernel,
        grid=(n // tn,),
        in_specs=[
            pl.BlockSpec((bsz, d), lambda j: (0, 0)),
            pl.BlockSpec((d, tn), lambda j: (0, j)),
            pl.BlockSpec((1, tn), lambda j: (0, j)),
        ],
        out_specs=pl.BlockSpec((bsz, tn), lambda j: (0, j)),
        out_shape=jax.ShapeDtypeStruct((bsz, n), F32),
        compiler_params=pltpu.CompilerParams(
            dimension_semantics=("arbitrary",), vmem_limit_bytes=VMEM_LIMIT),
        name="adaln_mod",
    )(c, w, b.reshape(1, n))


def _ffn_kernel(x_ref, shift_ref, scale_ref, gate_ref, gpre_ref, gpost_ref,
                wg_ref, wu_ref, wd_ref, o_ref, h_ref, acc_ref, *, coef):
    j = pl.program_id(1)

    @pl.when(j == 0)
    def _():
        _modulated_norm_rows(x_ref, gpre_ref, scale_ref, shift_ref, h_ref)

    h = h_ref[...]
    g = jnp.dot(h, wg_ref[...], preferred_element_type=F32)
    u = jnp.dot(h, wu_ref[...], preferred_element_type=F32)
    a = (jax.nn.silu(g) * u).astype(BF16)
    acc_ref[...] = (jnp.where(j > 0, acc_ref[...], 0.0)
                    + jnp.dot(a, wd_ref[...], preferred_element_type=F32))

    @pl.when(j == pl.num_programs(1) - 1)
    def _():
        _gated_residual_rows(x_ref, acc_ref, gpost_ref, gate_ref, coef, o_ref)


def _ffn_call(x, shift, scale, gate, g_pre, g_post, wg, wu, wd, coef, seq, tm=512, tf=512):
    t, d = x.shape
    ff = wg.shape[1]
    per_b = seq // tm
    row = lambda i, j: (i, 0)
    modm = lambda i, j: (i // per_b, 0, 0)
    const = lambda i, j: (0, 0)
    return pl.pallas_call(
        functools.partial(_ffn_kernel, coef=coef),
        grid=(t // tm, ff // tf),
        in_specs=[
            pl.BlockSpec((tm, d), row),
            pl.BlockSpec((None, 1, d), modm),
            pl.BlockSpec((None, 1, d), modm),
            pl.BlockSpec((None, 1, d), modm),
            pl.BlockSpec((1, d), const),
            pl.BlockSpec((1, d), const),
            pl.BlockSpec((d, tf), lambda i, j: (0, j)),
            pl.BlockSpec((d, tf), lambda i, j: (0, j)),
            pl.BlockSpec((tf, d), lambda i, j: (j, 0)),
        ],
        out_specs=pl.BlockSpec((tm, d), row),
        out_shape=jax.ShapeDtypeStruct((t, d), F32),
        scratch_shapes=[pltpu.VMEM((tm, d), BF16), pltpu.VMEM((tm, d), F32)],
        compiler_params=pltpu.CompilerParams(
            dimension_semantics=("parallel", "arbitrary"), vmem_limit_bytes=VMEM_LIMIT),
        name="ffn_sublayer",
    )(x, shift, scale, gate, g_pre, g_post, wg, wu, wd)


def _inproj_kernel(x_ref, shift_ref, scale_ref, gpre_ref, w_ref, wgate_ref,
                   o_ref, ogate_ref, h_ref):
    j = pl.program_id(1)

    @pl.when(j == 0)
    def _():
        _modulated_norm_rows(x_ref, gpre_ref, scale_ref, shift_ref, h_ref)
        ogate_ref[...] = jnp.dot(h_ref[...], wgate_ref[...], preferred_element_type=F32)

    o_ref[...] = jnp.dot(h_ref[...], w_ref[...], preferred_element_type=F32).astype(BF16)


def _inproj_call(x, shift, scale, g_pre, w_main, w_gate, seq, tm=1024, tn=1024):
    t, d = x.shape
    n = w_main.shape[1]
    per_b = seq // tm
    modm = lambda i, j: (i // per_b, 0, 0)
    return pl.pallas_call(
        _inproj_kernel,
        grid=(t // tm, n // tn),
        in_specs=[
            pl.BlockSpec((tm, d), lambda i, j: (i, 0)),
            pl.BlockSpec((None, 1, d), modm),
            pl.BlockSpec((None, 1, d), modm),
            pl.BlockSpec((1, d), lambda i, j: (0, 0)),
            pl.BlockSpec((d, tn), lambda i, j: (0, j)),
            pl.BlockSpec((d, LANES), lambda i, j: (0, 0)),
        ],
        out_specs=[
            pl.BlockSpec((tm, tn), lambda i, j: (i, j)),
            pl.BlockSpec((tm, LANES), lambda i, j: (i, 0)),
        ],
        out_shape=[jax.ShapeDtypeStruct((t, n), BF16), jax.ShapeDtypeStruct((t, LANES), F32)],
        scratch_shapes=[pltpu.VMEM((tm, d), BF16)],
        compiler_params=pltpu.CompilerParams(
            dimension_semantics=("parallel", "arbitrary"), vmem_limit_bytes=VMEM_LIMIT),
        name="mixer_inproj",
    )(x, shift, scale, g_pre, w_main, w_gate)


def _mixer_kernel(qk_ref, v_ref, o_ref, u_ref, gv_ref, gcol_ref, grow_ref, x_ref, gate_ref,
                  convw_ref, convb_ref, bcol_ref, brow_ref, gmh_ref, lng_ref, lnb_ref,
                  wsp_ref, bsp_ref, wout_ref, gpost_ref,
                  out_ref,
                  qkbuf, q_s, k_s, vaug_s, z_s, y_s, wc_s, c_s, m_s, *, ts):
    s_idx = pl.program_id(1)
    n_chunks = ts // CHUNK
    tail = SUBLANES

    @pl.when(s_idx == 0)
    def _():
        qkbuf[0:tail, :] = jnp.zeros((tail, 2 * D_QK), F32)
        c_s[...] = jnp.zeros_like(c_s)
        m_s[...] = jnp.zeros_like(m_s)
        row = lax.broadcasted_iota(jnp.int32, (CHUNK, CHUNK), 0)
        col = lax.broadcasted_iota(jnp.int32, (CHUNK, CHUNK), 1)
        for g in range(N_GROUPS):
            wc_s[g] = jnp.where(col <= row, wsp_ref[g], 0.0).astype(BF16)

    qkbuf[tail:tail + ts, :] = qk_ref[...].astype(F32)
    for c in range(n_chunks):
        base = tail + c * CHUNK - (CONV_WIDTH - 1)
        acc = convb_ref[...] + convw_ref[0:1, :] * qkbuf[base:base + CHUNK, :]
        for j in range(1, CONV_WIDTH):
            acc = acc + convw_ref[j:j + 1, :] * qkbuf[base + j:base + j + CHUNK, :]
        act = jax.nn.silu(acc)
        q_s[c * CHUNK:(c + 1) * CHUNK, :] = act[:, :D_QK].astype(BF16)
        k_s[c * CHUNK:(c + 1) * CHUNK, :] = act[:, D_QK:] * (DQK ** -0.5)
    qkbuf[0:tail, :] = qkbuf[ts:ts + tail, :]

    for h in range(N_HEADS):
        vaug_s[:, h * DV_AUG:h * DV_AUG + DV] = v_ref[:, h * DV:(h + 1) * DV]
        vaug_s[:, h * DV_AUG + DV:(h + 1) * DV_AUG] = jnp.ones((ts, LANES), BF16)

    row = lax.broadcasted_iota(jnp.int32, (CHUNK, CHUNK), 0)
    col = lax.broadcasted_iota(jnp.int32, (CHUNK, CHUNK), 1)
    causal = col <= row
    tril = jnp.where(causal, 1.0, 0.0).astype(F32)
    triu = jnp.where(row <= col, 1.0, 0.0).astype(F32)

    def chunk_body(c, carry):
        r0 = pl.multiple_of(c * CHUNK, CHUNK)
        rows = pl.ds(r0, CHUNK)

        g_col = gcol_ref[rows, :] + bcol_ref[...]
        g_row = grow_ref[c] + brow_ref[...]
        bcum_col = jnp.dot(tril, jax.nn.log_sigmoid(g_col), precision=lax.Precision.HIGHEST,
                           preferred_element_type=F32)
        bcum_row = jnp.dot(jax.nn.log_sigmoid(g_row), triu, precision=lax.Precision.HIGHEST,
                           preferred_element_type=F32)

        for h in range(N_HEADS):
            i_col = g_col[:, h:h + 1]
            b_col = bcum_col[:, N_HEADS + h:N_HEADS + h + 1]
            i_row = g_row[h:h + 1, :]
            b_row = bcum_row[N_HEADS + h:N_HEADS + h + 1, :]
            b_last = b_row[:, CHUNK - 1:CHUNK]
            m_prev = m_s[h][0:1, 0:1]

            q_h = q_s[rows, h * DQK:(h + 1) * DQK]
            k_h = k_s[rows, h * DQK:(h + 1) * DQK]
            va_h = vaug_s[rows, h * DV_AUG:(h + 1) * DV_AUG]

            dmat = jnp.where(causal, b_col - b_row + i_row, -jnp.inf)
            m_intra = jnp.max(dmat, axis=-1, keepdims=True)
            m_inter = b_col + m_prev
            m_t = jnp.maximum(m_inter, m_intra)
            inter_scale = jnp.exp(m_inter - m_t)
            p = jnp.exp(dmat - m_t)
            s_qk = lax.dot_general(q_h, k_h.astype(BF16), (((1,), (1,)), ((), ())),
                                   preferred_element_type=F32) * p
            nd = (inter_scale * jnp.dot(q_h, c_s[h].astype(BF16), preferred_element_type=F32)
                  + jnp.dot(s_qk.astype(BF16), va_h, preferred_element_type=F32))
            num = nd[:, :DV]
            den = nd[:, DV:DV + 1]
            hc = num / jnp.maximum(jnp.abs(den), jnp.exp(-m_t))
            hn = _rms(hc) * gmh_ref[:, h * DV:(h + 1) * DV]
            og = jax.nn.sigmoid(o_ref[rows, h * DV:(h + 1) * DV].astype(F32))
            z_s[rows, h * DV:(h + 1) * DV] = (og * hn).astype(BF16)

            w_row = b_last - b_row + i_row
            m_loc = jnp.max(w_row, axis=-1, keepdims=True)
            e_col = jnp.exp(b_last - b_col + i_col - m_loc)
            c_inc = lax.dot_general((k_h * e_col).astype(BF16), va_h, (((0,), (0,)), ((), ())),
                                    preferred_element_type=F32)
            m_new = jnp.maximum(b_last + m_prev, m_loc)
            c_s[h] = jnp.exp(b_last + m_prev - m_new) * c_s[h] + jnp.exp(m_loc - m_new) * c_inc
            m_s[h] = jnp.broadcast_to(m_new, (SUBLANES, LANES))

        gg = jax.nn.gelu(gv_ref[rows, :].astype(F32))
        mu = jnp.mean(gg, axis=-1, keepdims=True)
        var = jnp.mean(jnp.square(gg - mu), axis=-1, keepdims=True)
        vln = ((gg - mu) * lax.rsqrt(var + EPS) * lng_ref[...] + lnb_ref[...]).astype(BF16)
        for g in range(N_GROUPS):
            cols = slice(g * GROUP, (g + 1) * GROUP)
            zg = jnp.dot(wc_s[g], vln[:, cols], preferred_element_type=F32) + bsp_ref[:, g:g + 1]
            gu = jax.nn.gelu(u_ref[rows, cols].astype(F32))
            z_s[rows, D_MLSTM + g * GROUP:D_MLSTM + (g + 1) * GROUP] = (gu * zg).astype(BF16)
        return carry

    lax.fori_loop(0, n_chunks, chunk_body, 0)

    y_s[...] = jnp.dot(z_s[...], wout_ref[...], preferred_element_type=F32)
    _gated_residual_rows(x_ref, y_s, gpost_ref, gate_ref, 1.0, out_ref)


def _mixer_call(proj, g_col, g_row, x, gate, conv_w, conv_b, b_col, b_row, g_mh, ln_g, ln_b,
                w_sp, b_sp_t, w_out, g_post, bsz, seq, ts=512):
    t, d = x.shape
    nts = seq // ts
    nck = ts // CHUNK
    rowi = lambda k: (lambda b, s: (b * nts + s, k))
    const2 = lambda b, s: (0, 0)
    return pl.pallas_call(
        functools.partial(_mixer_kernel, ts=ts),
        grid=(bsz, nts),
        in_specs=[
            pl.BlockSpec((ts, 2 * D_QK), rowi(0)),
            pl.BlockSpec((ts, D_MLSTM), rowi(1)),
            pl.BlockSpec((ts, D_MLSTM), rowi(2)),
            pl.BlockSpec((ts, D_GMLP), rowi(3)),
            pl.BlockSpec((ts, D_GMLP), rowi(4)),
            pl.BlockSpec((ts, LANES), rowi(0)),
            pl.BlockSpec((None, nck, SUBLANES, CHUNK), lambda b, s: (b, s, 0, 0)),
            pl.BlockSpec((ts, d), rowi(0)),
            pl.BlockSpec((None, 1, d), lambda b, s: (b, 0, 0)),
            pl.BlockSpec((CONV_WIDTH, 2 * D_QK), const2),
            pl.BlockSpec((1, 2 * D_QK), const2),
            pl.BlockSpec((1, LANES), const2),
            pl.BlockSpec((SUBLANES, 1), const2),
            pl.BlockSpec((1, D_MLSTM), const2),
            pl.BlockSpec((1, D_GMLP), const2),
            pl.BlockSpec((1, D_GMLP), const2),
            pl.BlockSpec((N_GROUPS, CHUNK, CHUNK), lambda b, s: (0, 0, 0)),
            pl.BlockSpec((CHUNK, N_GROUPS), const2),
            pl.BlockSpec((d, d), const2),
            pl.BlockSpec((1, d), const2),
        ],
        out_specs=pl.BlockSpec((ts, d), rowi(0)),
        out_shape=jax.ShapeDtypeStruct((t, d), F32),
        scratch_shapes=[
            pltpu.VMEM((ts + 2 * SUBLANES, 2 * D_QK), F32),
            pltpu.VMEM((ts, D_QK), BF16),
            pltpu.VMEM((ts, D_QK), F32),
            pltpu.VMEM((ts, N_HEADS * DV_AUG), BF16),
            pltpu.VMEM((ts, d), BF16),
            pltpu.VMEM((ts, d), F32),
            pltpu.VMEM((N_GROUPS, CHUNK, CHUNK), BF16),
            pltpu.VMEM((N_HEADS, DQK, DV_AUG), F32),
            pltpu.VMEM((N_HEADS, SUBLANES, LANES), F32),
        ],
        compiler_params=pltpu.CompilerParams(
            dimension_semantics=("arbitrary", "arbitrary"), vmem_limit_bytes=VMEM_LIMIT),
        name="mixer_core",
    )(proj, proj, proj, proj, proj, g_col, g_row, x, gate, conv_w, conv_b, b_col, b_row,
      g_mh, ln_g, ln_b, w_sp, b_sp_t, w_out, g_post)


def kernel(x, c, w_ada, b_ada, g_pre, g_post, w_ff_gate, w_ff_up, w_ff_down, w_in, conv_w, conv_b,
           b_igate, b_fgate, g_mhnorm, gmlp_ln_g, gmlp_ln_b, w_spatial, b_spatial, w_out):
    bsz, seq, d = x.shape
    depth = w_ada.shape[0]
    xf = x.reshape(bsz * seq, d)
    for l in range(depth):
        mod = _ada_call(c, w_ada[l], b_ada[l]).reshape(bsz, N_SUB, N_MOD, 1, d)
        shift = lambda j: mod[:, j, 0]
        scale = lambda j: mod[:, j, 1]
        gate = lambda j: mod[:, j, 2]
        gp = lambda j: g_pre[l, j].reshape(1, d)
        gq = lambda j: g_post[l, j].reshape(1, d)

        def ffn(xin, j, idx):
            return _ffn_call(xin, shift(j), scale(j), gate(j), gp(j), gq(j),
                             w_ff_gate[l, idx].astype(BF16), w_ff_up[l, idx].astype(BF16),
                             w_ff_down[l, idx].astype(BF16), 0.5, seq)

        xf = ffn(xf, 0, 0)

        w_main = jnp.concatenate([w_in[l][:, :GATE_LO], w_in[l][:, GATE_HI:]], axis=1).astype(BF16)
        w_gate = jnp.pad(w_in[l][:, GATE_LO:GATE_HI], ((0, 0), (0, LANES - 2 * N_HEADS))).astype(BF16)
        proj, g_col = _inproj_call(xf, shift(1), scale(1), gp(1), w_main, w_gate, seq)
        g_row = (g_col[:, :SUBLANES].reshape(bsz, seq // CHUNK, CHUNK, SUBLANES)
                 .transpose(0, 1, 3, 2))
        bias = jnp.concatenate([b_igate[l], b_fgate[l]])
        b_col = jnp.pad(bias, (0, LANES - 2 * N_HEADS)).reshape(1, LANES)
        b_row = bias.reshape(SUBLANES, 1)
        xf = _mixer_call(proj, g_col, g_row, xf, gate(1), conv_w[l], conv_b[l].reshape(1, -1),
                         b_col, b_row, g_mhnorm[l].reshape(1, -1), gmlp_ln_g[l].reshape(1, -1),
                         gmlp_ln_b[l].reshape(1, -1), w_spatial[l], b_spatial[l].T,
                         w_out[l].astype(BF16), gq(1), bsz, seq)

        xf = ffn(xf, 2, 1)
    return xf.reshape(bsz, seq, d)
```

```python
import functools

import jax
import jax.numpy as jnp
from jax import lax
from jax.experimental import pallas as pl
from jax.experimental.pallas import tpu as pltpu

D_MODEL = 2048
D_MLSTM = D_MODEL // 2
N_HEADS = 4
DV = D_MLSTM // N_HEADS
DQK = DV // 2
D_QK = N_HEADS * DQK
CHUNK = 128
CONV_WIDTH = 4
D_GMLP = D_MODEL - D_MLSTM
N_GROUPS = 8
GROUP = D_GMLP // N_GROUPS
D_FF = 5632
N_SUB = 3
N_MOD = 3
EPS = 1e-6
GATE_LO = 2 * D_QK + 2 * D_MLSTM
GATE_HI = GATE_LO + 2 * N_HEADS
D_MAIN = 2 * D_QK + 2 * D_MLSTM + 2 * D_GMLP
LANES = 128
SUBLANES = 8
DV_AUG = DV + LANES

VMEM_LIMIT = 56 * 1024 * 1024

F32 = jnp.float32
BF16 = jnp.bfloat16


def _rms(y):
    return y * lax.rsqrt(jnp.mean(y * y, axis=-1, keepdims=True) + EPS)


ROW_BLOCK = 16
NORM_UNROLL = 16


def _modulated_norm_rows(x_ref, gpre_ref, scale_ref, shift_ref, h_ref):
    mult = gpre_ref[...] * (1.0 + scale_ref[...])
    shift = shift_ref[...]

    def body(r, carry):
        rows = pl.ds(pl.multiple_of(r * ROW_BLOCK, ROW_BLOCK), ROW_BLOCK)
        h_ref[rows, :] = (_rms(x_ref[rows, :]) * mult + shift).astype(h_ref.dtype)
        return carry

    lax.fori_loop(0, x_ref.shape[0] // ROW_BLOCK, body, 0, unroll=NORM_UNROLL)


def _gated_residual_rows(x_ref, y_ref, gpost_ref, gate_ref, coef, o_ref):
    mult = gpost_ref[...] * (coef * gate_ref[...])

    def body(r, carry):
        rows = pl.ds(pl.multiple_of(r * ROW_BLOCK, ROW_BLOCK), ROW_BLOCK)
        o_ref[rows, :] = x_ref[rows, :] + _rms(y_ref[rows, :]) * mult
        return carry

    lax.fori_loop(0, x_ref.shape[0] // ROW_BLOCK, body, 0, unroll=NORM_UNROLL)


def _ada_kernel(c_ref, w_ref, b_ref, o_ref):
    a = jax.nn.silu(c_ref[...]).astype(BF16)
    o_ref[...] = jnp.dot(a, w_ref[...].astype(BF16), preferred_element_type=F32) + b_ref[...]


def _ada_call(c, w, b, tn=1024):
    bsz, d = c.shape
    n = w.shape[1]
    return pl.pallas_call(
        _ada_kernel,
        grid=(n // tn,),
        in_specs=[
            pl.BlockSpec((bsz, d), lambda j: (0, 0)),
            pl.BlockSpec((d, tn), lambda j: (0, j)),
            pl.BlockSpec((1, tn), lambda j: (0, j)),
        ],
        out_specs=pl.BlockSpec((bsz, tn), lambda j: (0, j)),
        out_shape=jax.ShapeDtypeStruct((bsz, n), F32),
        compiler_params=pltpu.CompilerParams(
            dimension_semantics=("arbitrary",), vmem_limit_bytes=VMEM_LIMIT),
        name="adaln_mod",
    )(c, w, b.reshape(1, n))


def _ffn_kernel(x_ref, shift_ref, scale_ref, gate_ref, gpre_ref, gpost_ref,
                wg_ref, wu_ref, wd_ref, o_ref, h_ref, acc_ref, *, coef):
    j = pl.program_id(1)

    @pl.when(j == 0)
    def _():
        _modulated_norm_rows(x_ref, gpre_ref, scale_ref, shift_ref, h_ref)

    h = h_ref[...]
    g = jnp.dot(h, wg_ref[...], preferred_element_type=F32)
    u = jnp.dot(h, wu_ref[...], preferred_element_type=F32)
    a = (jax.nn.silu(g) * u).astype(BF16)
    acc_ref[...] = (jnp.where(j > 0, acc_ref[...], 0.0)
                    + jnp.dot(a, wd_ref[...], preferred_element_type=F32))

    @pl.when(j == pl.num_programs(1) - 1)
    def _():
        _gated_residual_rows(x_ref, acc_ref, gpost_ref, gate_ref, coef, o_ref)


def _ffn_call(x, shift, scale, gate, g_pre, g_post, wg, wu, wd, coef, seq, tm=512, tf=512):
    t, d = x.shape
    ff = wg.shape[1]
    per_b = seq // tm
    row = lambda i, j: (i, 0)
    modm = lambda i, j: (i // per_b, 0, 0)
    const = lambda i, j: (0, 0)
    return pl.pallas_call(
        functools.partial(_ffn_kernel, coef=coef),
        grid=(t // tm, ff // tf),
        in_specs=[
            pl.BlockSpec((tm, d), row),
            pl.BlockSpec((None, 1, d), modm),
            pl.BlockSpec((None, 1, d), modm),
            pl.BlockSpec((None, 1, d), modm),
            pl.BlockSpec((1, d), const),
            pl.BlockSpec((1, d), const),
            pl.BlockSpec((d, tf), lambda i, j: (0, j)),
            pl.BlockSpec((d, tf), lambda i, j: (0, j)),
            pl.BlockSpec((tf, d), lambda i, j: (j, 0)),
        ],
        out_specs=pl.BlockSpec((tm, d), row),
        out_shape=jax.ShapeDtypeStruct((t, d), F32),
        scratch_shapes=[pltpu.VMEM((tm, d), BF16), pltpu.VMEM((tm, d), F32)],
        compiler_params=pltpu.CompilerParams(
            dimension_semantics=("parallel", "arbitrary"), vmem_limit_bytes=VMEM_LIMIT),
        name="ffn_sublayer",
    )(x, shift, scale, gate, g_pre, g_post, wg, wu, wd)


def _inproj_kernel(x_ref, shift_ref, scale_ref, gpre_ref, w_ref, wgate_ref,
                   o_ref, ogate_ref, h_ref):
    j = pl.program_id(1)

    @pl.when(j == 0)
    def _():
        _modulated_norm_rows(x_ref, gpre_ref, scale_ref, shift_ref, h_ref)
        ogate_ref[...] = jnp.dot(h_ref[...], wgate_ref[...], preferred_element_type=F32)

    o_ref[...] = jnp.dot(h_ref[...], w_ref[...], preferred_element_type=F32).astype(BF16)


def _inproj_call(x, shift, scale, g_pre, w_main, w_gate, seq, tm=1024, tn=2560):
    t, d = x.shape
    n = w_main.shape[1]
    per_b = seq // tm
    modm = lambda i, j: (i // per_b, 0, 0)
    return pl.pallas_call(
        _inproj_kernel,
        grid=(t // tm, n // tn),
        in_specs=[
            pl.BlockSpec((tm, d), lambda i, j: (i, 0)),
            pl.BlockSpec((None, 1, d), modm),
            pl.BlockSpec((None, 1, d), modm),
            pl.BlockSpec((1, d), lambda i, j: (0, 0)),
            pl.BlockSpec((d, tn), lambda i, j: (0, j)),
            pl.BlockSpec((d, LANES), lambda i, j: (0, 0)),
        ],
        out_specs=[
            pl.BlockSpec((tm, tn), lambda i, j: (i, j)),
            pl.BlockSpec((tm, LANES), lambda i, j: (i, 0)),
        ],
        out_shape=[jax.ShapeDtypeStruct((t, n), BF16), jax.ShapeDtypeStruct((t, LANES), F32)],
        scratch_shapes=[pltpu.VMEM((tm, d), BF16)],
        compiler_params=pltpu.CompilerParams(
            dimension_semantics=("parallel", "arbitrary"), vmem_limit_bytes=VMEM_LIMIT),
        name="mixer_inproj",
    )(x, shift, scale, g_pre, w_main, w_gate)


def _mixer_kernel(qk_ref, v_ref, o_ref, u_ref, gv_ref, gcol_ref, grow_ref, x_ref, gate_ref,
                  convw_ref, convb_ref, bcol_ref, brow_ref, gmh_ref, lng_ref, lnb_ref,
                  wsp_ref, bsp_ref, wout_ref, gpost_ref,
                  out_ref,
                  qkbuf, q_s, k_s, vaug_s, z_s, y_s, wc_s, c_s, m_s, *, ts):
    s_idx = pl.program_id(1)
    n_chunks = ts // CHUNK
    tail = SUBLANES

    @pl.when(s_idx == 0)
    def _():
        qkbuf[0:tail, :] = jnp.zeros((tail, 2 * D_QK), F32)
        c_s[...] = jnp.zeros_like(c_s)
        m_s[...] = jnp.zeros_like(m_s)
        row = lax.broadcasted_iota(jnp.int32, (CHUNK, CHUNK), 0)
        col = lax.broadcasted_iota(jnp.int32, (CHUNK, CHUNK), 1)
        for g in range(N_GROUPS):
            wc_s[g] = jnp.where(col <= row, wsp_ref[g], 0.0).astype(BF16)

    qkbuf[tail:tail + ts, :] = qk_ref[...].astype(F32)
    for c in range(n_chunks):
        base = tail + c * CHUNK - (CONV_WIDTH - 1)
        acc = convb_ref[...] + convw_ref[0:1, :] * qkbuf[base:base + CHUNK, :]
        for j in range(1, CONV_WIDTH):
            acc = acc + convw_ref[j:j + 1, :] * qkbuf[base + j:base + j + CHUNK, :]
        act = jax.nn.silu(acc)
        q_s[c * CHUNK:(c + 1) * CHUNK, :] = act[:, :D_QK].astype(BF16)
        k_s[c * CHUNK:(c + 1) * CHUNK, :] = act[:, D_QK:] * (DQK ** -0.5)
    qkbuf[0:tail, :] = qkbuf[ts:ts + tail, :]

    for h in range(N_HEADS):
        vaug_s[:, h * DV_AUG:h * DV_AUG + DV] = v_ref[:, h * DV:(h + 1) * DV]
        vaug_s[:, h * DV_AUG + DV:(h + 1) * DV_AUG] = jnp.ones((ts, LANES), BF16)

    row = lax.broadcasted_iota(jnp.int32, (CHUNK, CHUNK), 0)
    col = lax.broadcasted_iota(jnp.int32, (CHUNK, CHUNK), 1)
    causal = col <= row
    tril = jnp.where(causal, 1.0, 0.0).astype(F32)
    triu = jnp.where(row <= col, 1.0, 0.0).astype(F32)

    def chunk_body(c, carry):
        r0 = pl.multiple_of(c * CHUNK, CHUNK)
        rows = pl.ds(r0, CHUNK)

        g_col = gcol_ref[rows, :] + bcol_ref[...]
        g_row = grow_ref[c] + brow_ref[...]
        bcum_col = jnp.dot(tril, jax.nn.log_sigmoid(g_col), precision=lax.Precision.HIGHEST,
                           preferred_element_type=F32)
        bcum_row = jnp.dot(jax.nn.log_sigmoid(g_row), triu, precision=lax.Precision.HIGHEST,
                           preferred_element_type=F32)

        for h in range(N_HEADS):
            i_col = g_col[:, h:h + 1]
            b_col = bcum_col[:, N_HEADS + h:N_HEADS + h + 1]
            i_row = g_row[h:h + 1, :]
            b_row = bcum_row[N_HEADS + h:N_HEADS + h + 1, :]
            b_last = b_row[:, CHUNK - 1:CHUNK]
            m_prev = m_s[h][0:1, 0:1]

            q_h = q_s[rows, h * DQK:(h + 1) * DQK]
            k_h = k_s[rows, h * DQK:(h + 1) * DQK]
            va_h = vaug_s[rows, h * DV_AUG:(h + 1) * DV_AUG]

            dmat = jnp.where(causal, b_col - b_row + i_row, -jnp.inf)
            m_intra = jnp.max(dmat, axis=-1, keepdims=True)
            m_inter = b_col + m_prev
            m_t = jnp.maximum(m_inter, m_intra)
            inter_scale = jnp.exp(m_inter - m_t)
            p = jnp.exp(dmat - m_t)
            s_qk = lax.dot_general(q_h, k_h.astype(BF16), (((1,), (1,)), ((), ())),
                                   preferred_element_type=F32) * p
            nd = (inter_scale * jnp.dot(q_h, c_s[h].astype(BF16), preferred_element_type=F32)
                  + jnp.dot(s_qk.astype(BF16), va_h, preferred_element_type=F32))
            num = nd[:, :DV]
            den = nd[:, DV:DV + 1]
            hc = num / jnp.maximum(jnp.abs(den), jnp.exp(-m_t))
            hn = _rms(hc) * gmh_ref[:, h * DV:(h + 1) * DV]
            og = jax.nn.sigmoid(o_ref[rows, h * DV:(h + 1) * DV].astype(F32))
            z_s[rows, h * DV:(h + 1) * DV] = (og * hn).astype(BF16)

            w_row = b_last - b_row + i_row
            m_loc = jnp.max(w_row, axis=-1, keepdims=True)
            e_col = jnp.exp(b_last - b_col + i_col - m_loc)
            c_inc = lax.dot_general((k_h * e_col).astype(BF16), va_h, (((0,), (0,)), ((), ())),
                                    preferred_element_type=F32)
            m_new = jnp.maximum(b_last + m_prev, m_loc)
            c_s[h] = jnp.exp(b_last + m_prev - m_new) * c_s[h] + jnp.exp(m_loc - m_new) * c_inc
            m_s[h] = jnp.broadcast_to(m_new, (SUBLANES, LANES))

        gg = jax.nn.gelu(gv_ref[rows, :].astype(F32))
        mu = jnp.mean(gg, axis=-1, keepdims=True)
        var = jnp.mean(jnp.square(gg - mu), axis=-1, keepdims=True)
        vln = ((gg - mu) * lax.rsqrt(var + EPS) * lng_ref[...] + lnb_ref[...]).astype(BF16)
        for g in range(N_GROUPS):
            cols = slice(g * GROUP, (g + 1) * GROUP)
            zg = jnp.dot(wc_s[g], vln[:, cols], preferred_element_type=F32) + bsp_ref[:, g:g + 1]
            gu = jax.nn.gelu(u_ref[rows, cols].astype(F32))
            z_s[rows, D_MLSTM + g * GROUP:D_MLSTM + (g + 1) * GROUP] = (gu * zg).astype(BF16)
        return carry

    lax.fori_loop(0, n_chunks, chunk_body, 0)

    y_s[...] = jnp.dot(z_s[...], wout_ref[...], preferred_element_type=F32)
    _gated_residual_rows(x_ref, y_s, gpost_ref, gate_ref, 1.0, out_ref)


def _mixer_call(proj, g_col, g_row, x, gate, conv_w, conv_b, b_col, b_row, g_mh, ln_g, ln_b,
                w_sp, b_sp_t, w_out, g_post, bsz, seq, ts=512):
    t, d = x.shape
    nts = seq // ts
    nck = ts // CHUNK
    rowi = lambda k: (lambda b, s: (b * nts + s, k))
    const2 = lambda b, s: (0, 0)
    return pl.pallas_call(
        functools.partial(_mixer_kernel, ts=ts),
        grid=(bsz, nts),
        in_specs=[
            pl.BlockSpec((ts, 2 * D_QK), rowi(0)),
            pl.BlockSpec((ts, D_MLSTM), rowi(1)),
            pl.BlockSpec((ts, D_MLSTM), rowi(2)),
            pl.BlockSpec((ts, D_GMLP), rowi(3)),
            pl.BlockSpec((ts, D_GMLP), rowi(4)),
            pl.BlockSpec((ts, LANES), rowi(0)),
            pl.BlockSpec((None, nck, SUBLANES, CHUNK), lambda b, s: (b, s, 0, 0)),
            pl.BlockSpec((ts, d), rowi(0)),
            pl.BlockSpec((None, 1, d), lambda b, s: (b, 0, 0)),
            pl.BlockSpec((CONV_WIDTH, 2 * D_QK), const2),
            pl.BlockSpec((1, 2 * D_QK), const2),
            pl.BlockSpec((1, LANES), const2),
            pl.BlockSpec((SUBLANES, 1), const2),
            pl.BlockSpec((1, D_MLSTM), const2),
            pl.BlockSpec((1, D_GMLP), const2),
            pl.BlockSpec((1, D_GMLP), const2),
            pl.BlockSpec((N_GROUPS, CHUNK, CHUNK), lambda b, s: (0, 0, 0)),
            pl.BlockSpec((CHUNK, N_GROUPS), const2),
            pl.BlockSpec((d, d), const2),
            pl.BlockSpec((1, d), const2),
        ],
        out_specs=pl.BlockSpec((ts, d), rowi(0)),
        out_shape=jax.ShapeDtypeStruct((t, d), F32),
        scratch_shapes=[
            pltpu.VMEM((ts + 2 * SUBLANES, 2 * D_QK), F32),
            pltpu.VMEM((ts, D_QK), BF16),
            pltpu.VMEM((ts, D_QK), F32),
            pltpu.VMEM((ts, N_HEADS * DV_AUG), BF16),
            pltpu.VMEM((ts, d), BF16),
            pltpu.VMEM((ts, d), F32),
            pltpu.VMEM((N_GROUPS, CHUNK, CHUNK), BF16),
            pltpu.VMEM((N_HEADS, DQK, DV_AUG), F32),
            pltpu.VMEM((N_HEADS, SUBLANES, LANES), F32),
        ],
        compiler_params=pltpu.CompilerParams(
            dimension_semantics=("arbitrary", "arbitrary"), vmem_limit_bytes=VMEM_LIMIT),
        name="mixer_core",
    )(proj, proj, proj, proj, proj, g_col, g_row, x, gate, conv_w, conv_b, b_col, b_row,
      g_mh, ln_g, ln_b, w_sp, b_sp_t, w_out, g_post)


def kernel(x, c, w_ada, b_ada, g_pre, g_post, w_ff_gate, w_ff_up, w_ff_down, w_in, conv_w, conv_b,
           b_igate, b_fgate, g_mhnorm, gmlp_ln_g, gmlp_ln_b, w_spatial, b_spatial, w_out):
    bsz, seq, d = x.shape
    depth = w_ada.shape[0]
    xf = x.reshape(bsz * seq, d)
    for l in range(depth):
        mod = _ada_call(c, w_ada[l], b_ada[l]).reshape(bsz, N_SUB, N_MOD, 1, d)
        shift = lambda j: mod[:, j, 0]
        scale = lambda j: mod[:, j, 1]
        gate = lambda j: mod[:, j, 2]
        gp = lambda j: g_pre[l, j].reshape(1, d)
        gq = lambda j: g_post[l, j].reshape(1, d)

        def ffn(xin, j, idx):
            return _ffn_call(xin, shift(j), scale(j), gate(j), gp(j), gq(j),
                             w_ff_gate[l, idx].astype(BF16), w_ff_up[l, idx].astype(BF16),
                             w_ff_down[l, idx].astype(BF16), 0.5, seq)

        xf = ffn(xf, 0, 0)

        w_main = jnp.concatenate([w_in[l][:, :GATE_LO], w_in[l][:, GATE_HI:]], axis=1).astype(BF16)
        w_gate = jnp.pad(w_in[l][:, GATE_LO:GATE_HI], ((0, 0), (0, LANES - 2 * N_HEADS))).astype(BF16)
        proj, g_col = _inproj_call(xf, shift(1), scale(1), gp(1), w_main, w_gate, seq)
        g_row = (g_col[:, :SUBLANES].reshape(bsz, seq // CHUNK, CHUNK, SUBLANES)
                 .transpose(0, 1, 3, 2))
        bias = jnp.concatenate([b_igate[l], b_fgate[l]])
        b_col = jnp.pad(bias, (0, LANES - 2 * N_HEADS)).reshape(1, LANES)
        b_row = bias.reshape(SUBLANES, 1)
        xf = _mixer_call(proj, g_col, g_row, xf, gate(1), conv_w[l], conv_b[l].reshape(1, -1),
                         b_col, b_row, g_mhnorm[l].reshape(1, -1), gmlp_ln_g[l].reshape(1, -1),
                         gmlp_ln_b[l].reshape(1, -1), w_spatial[l], b_spatial[l].T,
                         w_out[l].astype(BF16), gq(1), bsz, seq)

        xf = ffn(xf, 2, 1)
    return xf.reshape(bsz, seq, d)
```

```python
import functools

import jax
import jax.numpy as jnp
from jax import lax
from jax.experimental import pallas as pl
from jax.experimental.pallas import tpu as pltpu

D_MODEL = 2048
D_MLSTM = D_MODEL // 2
N_HEADS = 4
DV = D_MLSTM // N_HEADS
DQK = DV // 2
D_QK = N_HEADS * DQK
CHUNK = 128
CONV_WIDTH = 4
D_GMLP = D_MODEL - D_MLSTM
N_GROUPS = 8
GROUP = D_GMLP // N_GROUPS
D_FF = 5632
N_SUB = 3
N_MOD = 3
EPS = 1e-6
GATE_LO = 2 * D_QK + 2 * D_MLSTM
GATE_HI = GATE_LO + 2 * N_HEADS
D_MAIN = 2 * D_QK + 2 * D_MLSTM + 2 * D_GMLP
LANES = 128
SUBLANES = 8
DV_AUG = DV + LANES

VMEM_LIMIT = 56 * 1024 * 1024

F32 = jnp.float32
BF16 = jnp.bfloat16


def _rms(y):
    return y * lax.rsqrt(jnp.mean(y * y, axis=-1, keepdims=True) + EPS)


ROW_BLOCK = 16
NORM_UNROLL = 16


def _modulated_norm_rows(x_ref, gpre_ref, scale_ref, shift_ref, h_ref):
    mult = gpre_ref[...] * (1.0 + scale_ref[...])
    shift = shift_ref[...]

    def body(r, carry):
        rows = pl.ds(pl.multiple_of(r * ROW_BLOCK, ROW_BLOCK), ROW_BLOCK)
        h_ref[rows, :] = (_rms(x_ref[rows, :]) * mult + shift).astype(h_ref.dtype)
        return carry

    lax.fori_loop(0, x_ref.shape[0] // ROW_BLOCK, body, 0, unroll=NORM_UNROLL)


def _gated_residual_rows(x_ref, y_ref, gpost_ref, gate_ref, coef, o_ref):
    mult = gpost_ref[...] * (coef * gate_ref[...])

    def body(r, carry):
        rows = pl.ds(pl.multiple_of(r * ROW_BLOCK, ROW_BLOCK), ROW_BLOCK)
        o_ref[rows, :] = x_ref[rows, :] + _rms(y_ref[rows, :]) * mult
        return carry

    lax.fori_loop(0, x_ref.shape[0] // ROW_BLOCK, body, 0, unroll=NORM_UNROLL)


def _ada_kernel(c_ref, w_ref, b_ref, o_ref):
    a = jax.nn.silu(c_ref[...]).astype(BF16)
    o_ref[...] = jnp.dot(a, w_ref[...].astype(BF16), preferred_element_type=F32) + b_ref[...]


def _ada_call(c, w, b, tn=1024):
    bsz, d = c.shape
    n = w.shape[1]
    return pl.pallas_call(
        _ada_kernel,
        grid=(n // tn,),
        in_specs=[
            pl.BlockSpec((bsz, d), lambda j: (0, 0)),
            pl.BlockSpec((d, tn), lambda j: (0, j)),
            pl.BlockSpec((1, tn), lambda j: (0, j)),
        ],
        out_specs=pl.BlockSpec((bsz, tn), lambda j: (0, j)),
        out_shape=jax.ShapeDtypeStruct((bsz, n), F32),
        compiler_params=pltpu.CompilerParams(
            dimension_semantics=("arbitrary",), vmem_limit_bytes=VMEM_LIMIT),
        name="adaln_mod",
    )(c, w, b.reshape(1, n))


def _ffn_kernel(x_ref, shift_ref, scale_ref, gate_ref, gpre_ref, gpost_ref,
                wg_ref, wu_ref, wd_ref, o_ref, h_ref, acc_ref, *, coef):
    j = pl.program_id(1)

    @pl.when(j == 0)
    def _():
        _modulated_norm_rows(x_ref, gpre_ref, scale_ref, shift_ref, h_ref)

    h = h_ref[...]
    g = jnp.dot(h, wg_ref[...], preferred_element_type=F32)
    u = jnp.dot(h, wu_ref[...], preferred_element_type=F32)
    a = (jax.nn.silu(g) * u).astype(BF16)
    acc_ref[...] = (jnp.where(j > 0, acc_ref[...], 0.0)
                    + jnp.dot(a, wd_ref[...], preferred_element_type=F32))

    @pl.when(j == pl.num_programs(1) - 1)
    def _():
        _gated_residual_rows(x_ref, acc_ref, gpost_ref, gate_ref, coef, o_ref)


def _ffn_call(x, shift, scale, gate, g_pre, g_post, wg, wu, wd, coef, seq, tm=512, tf=512):
    t, d = x.shape
    ff = wg.shape[1]
    per_b = seq // tm
    row = lambda i, j: (i, 0)
    modm = lambda i, j: (i // per_b, 0, 0)
    const = lambda i, j: (0, 0)
    return pl.pallas_call(
        functools.partial(_ffn_kernel, coef=coef),
        grid=(t // tm, ff // tf),
        in_specs=[
            pl.BlockSpec((tm, d), row),
            pl.BlockSpec((None, 1, d), modm),
            pl.BlockSpec((None, 1, d), modm),
            pl.BlockSpec((None, 1, d), modm),
            pl.BlockSpec((1, d), const),
            pl.BlockSpec((1, d), const),
            pl.BlockSpec((d, tf), lambda i, j: (0, j)),
            pl.BlockSpec((d, tf), lambda i, j: (0, j)),
            pl.BlockSpec((tf, d), lambda i, j: (j, 0)),
        ],
        out_specs=pl.BlockSpec((tm, d), row),
        out_shape=jax.ShapeDtypeStruct((t, d), F32),
        scratch_shapes=[pltpu.VMEM((tm, d), BF16), pltpu.VMEM((tm, d), F32)],
        compiler_params=pltpu.CompilerParams(
            dimension_semantics=("parallel", "arbitrary"), vmem_limit_bytes=VMEM_LIMIT),
        name="ffn_sublayer",
    )(x, shift, scale, gate, g_pre, g_post, wg, wu, wd)


def _inproj_kernel(x_ref, shift_ref, scale_ref, gpre_ref, w_ref, wgate_ref,
                   o_ref, ogate_ref, h_ref):
    j = pl.program_id(1)

    @pl.when(j == 0)
    def _():
        _modulated_norm_rows(x_ref, gpre_ref, scale_ref, shift_ref, h_ref)
        ogate_ref[...] = jnp.dot(h_ref[...], wgate_ref[...], preferred_element_type=F32)

    o_ref[...] = jnp.dot(h_ref[...], w_ref[...], preferred_element_type=F32).astype(BF16)


def _inproj_call(x, shift, scale, g_pre, w_main, w_gate, seq, tm=1024, tn=2560):
    t, d = x.shape
    n = w_main.shape[1]
    per_b = seq // tm
    modm = lambda i, j: (i // per_b, 0, 0)
    return pl.pallas_call(
        _inproj_kernel,
        grid=(t // tm, n // tn),
        in_specs=[
            pl.BlockSpec((tm, d), lambda i, j: (i, 0)),
            pl.BlockSpec((None, 1, d), modm),
            pl.BlockSpec((None, 1, d), modm),
            pl.BlockSpec((1, d), lambda i, j: (0, 0)),
            pl.BlockSpec((d, tn), lambda i, j: (0, j)),
            pl.BlockSpec((d, LANES), lambda i, j: (0, 0)),
        ],
        out_specs=[
            pl.BlockSpec((tm, tn), lambda i, j: (i, j)),
            pl.BlockSpec((tm, LANES), lambda i, j: (i, 0)),
        ],
        out_shape=[jax.ShapeDtypeStruct((t, n), BF16), jax.ShapeDtypeStruct((t, LANES), F32)],
        scratch_shapes=[pltpu.VMEM((tm, d), BF16)],
        compiler_params=pltpu.CompilerParams(
            dimension_semantics=("parallel", "arbitrary"), vmem_limit_bytes=VMEM_LIMIT),
        name="mixer_inproj",
    )(x, shift, scale, g_pre, w_main, w_gate)


def _mixer_kernel(qk_ref, v_ref, o_ref, u_ref, gv_ref, gcol_ref, grow_ref, x_ref, gate_ref,
                  convw_ref, convb_ref, bcol_ref, brow_ref, gmh_ref, lng_ref, lnb_ref,
                  wsp_ref, bsp_ref, wout_ref, gpost_ref,
                  out_ref,
                  qkbuf, q_s, k_s, vaug_s, z_s, y_s, wc_s, c_s, m_s, *, ts):
    s_idx = pl.program_id(1)
    n_chunks = ts // CHUNK
    tail = SUBLANES

    @pl.when(s_idx == 0)
    def _():
        qkbuf[0:tail, :] = jnp.zeros((tail, 2 * D_QK), F32)
        qkbuf[tail + ts:2 * tail + ts, :] = jnp.zeros((tail, 2 * D_QK), F32)
        c_s[...] = jnp.zeros_like(c_s)
        m_s[...] = jnp.zeros_like(m_s)
        row = lax.broadcasted_iota(jnp.int32, (CHUNK, CHUNK), 0)
        col = lax.broadcasted_iota(jnp.int32, (CHUNK, CHUNK), 1)
        for g in range(N_GROUPS):
            wc_s[g] = jnp.where(col <= row, wsp_ref[g], 0.0).astype(BF16)

    qkbuf[tail:tail + ts, :] = qk_ref[...].astype(F32)
    win = CHUNK + 2 * tail
    t_i = lax.broadcasted_iota(jnp.int32, (CHUNK, win), 0)
    r_i = lax.broadcasted_iota(jnp.int32, (CHUNK, win), 1)
    shifts = [jnp.where(r_i == t_i + (tail - (CONV_WIDTH - 1) + j), 1.0, 0.0).astype(BF16)
              for j in range(CONV_WIDTH)]
    for c in range(n_chunks):
        xe = qkbuf[c * CHUNK:c * CHUNK + win, :].astype(BF16)
        acc = convb_ref[...]
        for j in range(CONV_WIDTH):
            acc = acc + convw_ref[j:j + 1, :] * jnp.dot(shifts[j], xe, preferred_element_type=F32)
        act = jax.nn.silu(acc)
        q_s[c * CHUNK:(c + 1) * CHUNK, :] = act[:, :D_QK].astype(BF16)
        k_s[c * CHUNK:(c + 1) * CHUNK, :] = act[:, D_QK:] * (DQK ** -0.5)
    qkbuf[0:tail, :] = qkbuf[ts:ts + tail, :]

    for h in range(N_HEADS):
        vaug_s[:, h * DV_AUG:h * DV_AUG + DV] = v_ref[:, h * DV:(h + 1) * DV]
        vaug_s[:, h * DV_AUG + DV:(h + 1) * DV_AUG] = jnp.ones((ts, LANES), BF16)

    row = lax.broadcasted_iota(jnp.int32, (CHUNK, CHUNK), 0)
    col = lax.broadcasted_iota(jnp.int32, (CHUNK, CHUNK), 1)
    causal = col <= row
    tril = jnp.where(causal, 1.0, 0.0).astype(F32)
    triu = jnp.where(row <= col, 1.0, 0.0).astype(F32)

    def chunk_body(c, carry):
        r0 = pl.multiple_of(c * CHUNK, CHUNK)
        rows = pl.ds(r0, CHUNK)

        g_col = gcol_ref[rows, :] + bcol_ref[...]
        g_row = grow_ref[c] + brow_ref[...]
        bcum_col = jnp.dot(tril, jax.nn.log_sigmoid(g_col), precision=lax.Precision.HIGHEST,
                           preferred_element_type=F32)
        bcum_row = jnp.dot(jax.nn.log_sigmoid(g_row), triu, precision=lax.Precision.HIGHEST,
                           preferred_element_type=F32)

        for h in range(N_HEADS):
            i_col = g_col[:, h:h + 1]
            b_col = bcum_col[:, N_HEADS + h:N_HEADS + h + 1]
            i_row = g_row[h:h + 1, :]
            b_row = bcum_row[N_HEADS + h:N_HEADS + h + 1, :]
            b_last = b_row[:, CHUNK - 1:CHUNK]
            m_prev = m_s[h][0:1, 0:1]

            q_h = q_s[rows, h * DQK:(h + 1) * DQK]
            k_h = k_s[rows, h * DQK:(h + 1) * DQK]
            va_h = vaug_s[rows, h * DV_AUG:(h + 1) * DV_AUG]

            dmat = jnp.where(causal, b_col - b_row + i_row, -jnp.inf)
            m_intra = jnp.max(dmat, axis=-1, keepdims=True)
            m_inter = b_col + m_prev
            m_t = jnp.maximum(m_inter, m_intra)
            inter_scale = jnp.exp(m_inter - m_t)
            p = jnp.exp(dmat - m_t)
            s_qk = lax.dot_general(q_h, k_h.astype(BF16), (((1,), (1,)), ((), ())),
                                   preferred_element_type=F32) * p
            nd = (inter_scale * jnp.dot(q_h, c_s[h].astype(BF16), preferred_element_type=F32)
                  + jnp.dot(s_qk.astype(BF16), va_h, preferred_element_type=F32))
            num = nd[:, :DV]
            den = nd[:, DV:DV + 1]
            hc = num / jnp.maximum(jnp.abs(den), jnp.exp(-m_t))
            hn = _rms(hc) * gmh_ref[:, h * DV:(h + 1) * DV]
            og = jax.nn.sigmoid(o_ref[rows, h * DV:(h + 1) * DV].astype(F32))
            z_s[rows, h * DV:(h + 1) * DV] = (og * hn).astype(BF16)

            w_row = b_last - b_row + i_row
            m_loc = jnp.max(w_row, axis=-1, keepdims=True)
            e_col = jnp.exp(b_last - b_col + i_col - m_loc)
            c_inc = lax.dot_general((k_h * e_col).astype(BF16), va_h, (((0,), (0,)), ((), ())),
                                    preferred_element_type=F32)
            m_new = jnp.maximum(b_last + m_prev, m_loc)
            c_s[h] = jnp.exp(b_last + m_prev - m_new) * c_s[h] + jnp.exp(m_loc - m_new) * c_inc
            m_s[h] = jnp.broadcast_to(m_new, (SUBLANES, LANES))

        gg = jax.nn.gelu(gv_ref[rows, :].astype(F32))
        mu = jnp.mean(gg, axis=-1, keepdims=True)
        var = jnp.mean(jnp.square(gg - mu), axis=-1, keepdims=True)
        vln = ((gg - mu) * lax.rsqrt(var + EPS) * lng_ref[...] + lnb_ref[...]).astype(BF16)
        for g in range(N_GROUPS):
            cols = slice(g * GROUP, (g + 1) * GROUP)
            zg = jnp.dot(wc_s[g], vln[:, cols], preferred_element_type=F32) + bsp_ref[:, g:g + 1]
            gu = jax.nn.gelu(u_ref[rows, cols].astype(F32))
            z_s[rows, D_MLSTM + g * GROUP:D_MLSTM + (g + 1) * GROUP] = (gu * zg).astype(BF16)
        return carry

    lax.fori_loop(0, n_chunks, chunk_body, 0)

    y_s[...] = jnp.dot(z_s[...], wout_ref[...], preferred_element_type=F32)
    _gated_residual_rows(x_ref, y_s, gpost_ref, gate_ref, 1.0, out_ref)


def _mixer_call(proj, g_col, g_row, x, gate, conv_w, conv_b, b_col, b_row, g_mh, ln_g, ln_b,
                w_sp, b_sp_t, w_out, g_post, bsz, seq, ts=512):
    t, d = x.shape
    nts = seq // ts
    nck = ts // CHUNK
    rowi = lambda k: (lambda b, s: (b * nts + s, k))
    const2 = lambda b, s: (0, 0)
    return pl.pallas_call(
        functools.partial(_mixer_kernel, ts=ts),
        grid=(bsz, nts),
        in_specs=[
            pl.BlockSpec((ts, 2 * D_QK), rowi(0)),
            pl.BlockSpec((ts, D_MLSTM), rowi(1)),
            pl.BlockSpec((ts, D_MLSTM), rowi(2)),
            pl.BlockSpec((ts, D_GMLP), rowi(3)),
            pl.BlockSpec((ts, D_GMLP), rowi(4)),
            pl.BlockSpec((ts, LANES), rowi(0)),
            pl.BlockSpec((None, nck, SUBLANES, CHUNK), lambda b, s: (b, s, 0, 0)),
            pl.BlockSpec((ts, d), rowi(0)),
            pl.BlockSpec((None, 1, d), lambda b, s: (b, 0, 0)),
            pl.BlockSpec((CONV_WIDTH, 2 * D_QK), const2),
            pl.BlockSpec((1, 2 * D_QK), const2),
            pl.BlockSpec((1, LANES), const2),
            pl.BlockSpec((SUBLANES, 1), const2),
            pl.BlockSpec((1, D_MLSTM), const2),
            pl.BlockSpec((1, D_GMLP), const2),
            pl.BlockSpec((1, D_GMLP), const2),
            pl.BlockSpec((N_GROUPS, CHUNK, CHUNK), lambda b, s: (0, 0, 0)),
            pl.BlockSpec((CHUNK, N_GROUPS), const2),
            pl.BlockSpec((d, d), const2),
            pl.BlockSpec((1, d), const2),
        ],
        out_specs=pl.BlockSpec((ts, d), rowi(0)),
        out_shape=jax.ShapeDtypeStruct((t, d), F32),
        scratch_shapes=[
            pltpu.VMEM((ts + 2 * SUBLANES, 2 * D_QK), F32),
            pltpu.VMEM((ts, D_QK), BF16),
            pltpu.VMEM((ts, D_QK), F32),
            pltpu.VMEM((ts, N_HEADS * DV_AUG), BF16),
            pltpu.VMEM((ts, d), BF16),
            pltpu.VMEM((ts, d), F32),
            pltpu.VMEM((N_GROUPS, CHUNK, CHUNK), BF16),
            pltpu.VMEM((N_HEADS, DQK, DV_AUG), F32),
            pltpu.VMEM((N_HEADS, SUBLANES, LANES), F32),
        ],
        compiler_params=pltpu.CompilerParams(
            dimension_semantics=("arbitrary", "arbitrary"), vmem_limit_bytes=VMEM_LIMIT),
        name="mixer_core",
    )(proj, proj, proj, proj, proj, g_col, g_row, x, gate, conv_w, conv_b, b_col, b_row,
      g_mh, ln_g, ln_b, w_sp, b_sp_t, w_out, g_post)


def kernel(x, c, w_ada, b_ada, g_pre, g_post, w_ff_gate, w_ff_up, w_ff_down, w_in, conv_w, conv_b,
           b_igate, b_fgate, g_mhnorm, gmlp_ln_g, gmlp_ln_b, w_spatial, b_spatial, w_out):
    bsz, seq, d = x.shape
    depth = w_ada.shape[0]
    xf = x.reshape(bsz * seq, d)
    for l in range(depth):
        mod = _ada_call(c, w_ada[l], b_ada[l]).reshape(bsz, N_SUB, N_MOD, 1, d)
        shift = lambda j: mod[:, j, 0]
        scale = lambda j: mod[:, j, 1]
        gate = lambda j: mod[:, j, 2]
        gp = lambda j: g_pre[l, j].reshape(1, d)
        gq = lambda j: g_post[l, j].reshape(1, d)

        def ffn(xin, j, idx):
            return _ffn_call(xin, shift(j), scale(j), gate(j), gp(j), gq(j),
                             w_ff_gate[l, idx].astype(BF16), w_ff_up[l, idx].astype(BF16),
                             w_ff_down[l, idx].astype(BF16), 0.5, seq)

        xf = ffn(xf, 0, 0)

        w_main = jnp.concatenate([w_in[l][:, :GATE_LO], w_in[l][:, GATE_HI:]], axis=1).astype(BF16)
        w_gate = jnp.pad(w_in[l][:, GATE_LO:GATE_HI], ((0, 0), (0, LANES - 2 * N_HEADS))).astype(BF16)
        proj, g_col = _inproj_call(xf, shift(1), scale(1), gp(1), w_main, w_gate, seq)
        g_row = (g_col[:, :SUBLANES].reshape(bsz, seq // CHUNK, CHUNK, SUBLANES)
                 .transpose(0, 1, 3, 2))
        bias = jnp.concatenate([b_igate[l], b_fgate[l]])
        b_col = jnp.pad(bias, (0, LANES - 2 * N_HEADS)).reshape(1, LANES)
        b_row = bias.reshape(SUBLANES, 1)
        xf = _mixer_call(proj, g_col, g_row, xf, gate(1), conv_w[l], conv_b[l].reshape(1, -1),
                         b_col, b_row, g_mhnorm[l].reshape(1, -1), gmlp_ln_g[l].reshape(1, -1),
                         gmlp_ln_b[l].reshape(1, -1), w_spatial[l], b_spatial[l].T,
                         w_out[l].astype(BF16), gq(1), bsz, seq)

        xf = ffn(xf, 2, 1)
    return xf.reshape(bsz, seq, d)
```
